```python
import math, functools
import jax, jax.numpy as jnp
from jax import lax
import numpy as np

D_MODEL = 4096
BATCH = 4
SEQ = 2048
DEPTH = 4
DEC_BATCH = 128
DEC_SEQ = 4
PAST_LEN = 8192
PAGE_SIZE = 128

N_EVEN = (DEPTH + 1) // 2
N_ODD = DEPTH // 2
D_A = D_MODEL // 2
A_GROUPS = 16
A_WIDTH = 3
N_HEADS = 16
QK_NOPE = 128
QK_ROPE = 64
V_DIM = 128
Q_LORA = D_MODEL // 4
KV_LORA = D_MODEL // 8
ROPE_THETA = 10000.0
SM_SCALE = (QK_NOPE + QK_ROPE) ** -0.5
Q_BLOCK = 128
IN_E = 3 * D_A + Q_LORA + KV_LORA + QK_ROPE
MIX_E = D_A + N_HEADS * V_DIM
D_CONF = D_MODEL
CONF_WIDTH = 31
D_FF = ((8 * D_MODEL // 3 + 255) // 256) * 256
FF_WIDTH = 3
EPS = 1e-6

kernel_name = "hybrid_shortconv_mla_conformer_convffn_step"


def rmsnorm(x, g):
    xf = x.astype(jnp.float32)
    y = xf * lax.rsqrt(jnp.mean(xf * xf, axis=-1, keepdims=True) + EPS)
    return (y * g.astype(jnp.float32)).astype(x.dtype)


def layernorm(x, g, b):
    xf = x.astype(jnp.float32)
    mu = jnp.mean(xf, axis=-1, keepdims=True)
    xc = xf - mu
    y = xc * lax.rsqrt(jnp.mean(xc * xc, axis=-1, keepdims=True) + EPS)
    return (y * g.astype(jnp.float32) + b.astype(jnp.float32)).astype(x.dtype)


def causal_dwconv(full, w):
    k_w = w.shape[0]
    L = full.shape[1] - k_w + 1
    out = full[:, 0:L] * w[0]
    for k in range(1, k_w):
        out = out + full[:, k:k + L] * w[k]
    return out


def rope(x, pos):
    half = QK_ROPE // 2
    inv = 1.0 / (ROPE_THETA ** (jnp.arange(half, dtype=jnp.float32) * (2.0 / QK_ROPE)))
    ang = pos.astype(jnp.float32)[:, None] * inv[None, :]
    cos = jnp.cos(ang)[None, :, None, :].astype(x.dtype)
    sin = jnp.sin(ang)[None, :, None, :].astype(x.dtype)
    x1, x2 = x[..., :half], x[..., half:]
    return jnp.concatenate([x1 * cos - x2 * sin, x2 * cos + x1 * sin], axis=-1)


def attn_prompt(q_lat, q_pe, ckv, kpe):
    B, S, H, C = q_lat.shape
    nb = S // Q_BLOCK
    qb = q_lat.reshape(B, nb, Q_BLOCK, H, C).transpose(1, 0, 2, 3, 4)
    pb = q_pe.reshape(B, nb, Q_BLOCK, H, QK_ROPE).transpose(1, 0, 2, 3, 4)
    k_pos = jnp.arange(S)

    def block(args):
        ql, qp, i = args
        q_pos = i * Q_BLOCK + jnp.arange(Q_BLOCK)
        s = jnp.einsum('bqhc,bkc->bhqk', ql, ckv) + jnp.einsum('bqhr,bkr->bhqk', qp, kpe)
        s = s.astype(jnp.float32) * SM_SCALE
        s = jnp.where(k_pos[None, :] <= q_pos[:, None], s, -jnp.inf)
        p = jax.nn.softmax(s, axis=-1).astype(ckv.dtype)
        return jnp.einsum('bhqk,bkc->bqhc', p, ckv)

    out = lax.map(block, (qb, pb, jnp.arange(nb)))
    return out.transpose(1, 0, 2, 3, 4).reshape(B, S, H, C)


def attn_sample(q_lat, q_pe, ckv, kpe, cache_ckv, cache_kpe, page_table, layer):
    ckv_past = cache_ckv[layer, page_table]
    kpe_past = cache_kpe[layer, page_table]
    Bd, T = q_lat.shape[0], q_lat.shape[1]
    n_pages, pg = ckv_past.shape[1], ckv_past.shape[2]
    P = n_pages * pg
    s_past = (jnp.einsum('bthc,bnpc->bhtnp', q_lat, ckv_past)
              + jnp.einsum('bthr,bnpr->bhtnp', q_pe, kpe_past)).reshape(Bd, N_HEADS, T, P)
    s_new = jnp.einsum('bthc,buc->bhtu', q_lat, ckv) + jnp.einsum('bthr,bur->bhtu', q_pe, kpe)
    s_past = s_past.astype(jnp.float32) * SM_SCALE
    s_new = s_new.astype(jnp.float32) * SM_SCALE
    t_idx = jnp.arange(T)
    s_new = jnp.where(t_idx[None, :] <= t_idx[:, None], s_new, -jnp.inf)
    p = jax.nn.softmax(jnp.concatenate([s_past, s_new], axis=-1), axis=-1).astype(ckv.dtype)
    p_past = p[..., :P].reshape(Bd, N_HEADS, T, n_pages, pg)
    p_new = p[..., P:]
    return (jnp.einsum('bhtnp,bnpc->bthc', p_past, ckv_past)
            + jnp.einsum('bhtu,buc->bthc', p_new, ckv))


def even_mixer(h, pos, hist, attn, w_in, conv_w, qn, wuq, kvn, wukv, wout):
    B, L, _ = h.shape
    z = h @ w_in
    zb, zc, zx, zq, zkv = jnp.split(z, [D_A, 2 * D_A, 3 * D_A, 3 * D_A + Q_LORA], axis=-1)
    u = zc * zx
    full = jnp.concatenate([hist, u], axis=1)
    ya = zb * causal_dwconv(full, conv_w)
    new_hist = full[:, -(A_WIDTH - 1):]
    q = (rmsnorm(zq, qn) @ wuq).reshape(B, L, N_HEADS, QK_NOPE + QK_ROPE)
    q_nope = q[..., :QK_NOPE]
    q_pe = rope(q[..., QK_NOPE:], pos)
    ckv = rmsnorm(zkv[..., :KV_LORA], kvn)
    kpe = rope(zkv[..., KV_LORA:][:, :, None, :], pos)[:, :, 0]
    w_uk = wukv[..., :QK_NOPE]
    w_uv = wukv[..., QK_NOPE:]
    q_lat = jnp.einsum('blhd,chd->blhc', q_nope, w_uk)
    o_lat = attn(q_lat, q_pe, ckv, kpe)
    o = jnp.einsum('blhc,chd->blhd', o_lat, w_uv).reshape(B, L, N_HEADS * V_DIM)
    y = jnp.concatenate([ya, o], axis=-1) @ wout
    return y, new_hist, ckv, kpe


def conformer_mixer(h, hist, w1, b1, cw, cb, lg, lb, w2, b2):
    a = h @ w1 + b1
    g = a[..., :D_CONF] * jax.nn.sigmoid(a[..., D_CONF:])
    full = jnp.concatenate([hist, g], axis=1)
    c = causal_dwconv(full, cw) + cb
    c = jax.nn.silu(layernorm(c, lg, lb))
    return c @ w2 + b2, full[:, -(CONF_WIDTH - 1):]


def conv_ffn(h, hist, w_up, cw, w_down):
    u = h @ w_up
    full = jnp.concatenate([hist, u], axis=1)
    c = causal_dwconv(full, cw)
    y = (jax.nn.silu(c[..., :D_FF]) * c[..., D_FF:]) @ w_down
    return y, full[:, -(FF_WIDTH - 1):]


def trunk(x, pos, h_sconv, h_conf, h_ffn, attn_fns, p):
    sc_new, cf_new, ff_new, ckvs, kpes = [], [], [], [], []
    for l in range(DEPTH):
        h = rmsnorm(x, p['norm_mix'][l])
        if l % 2 == 0:
            e = l // 2
            y, hs, ckv, kpe = even_mixer(h, pos, h_sconv[e], attn_fns[e], p['w_in_e'][e], p['conv_a'][e],
                                         p['q_norm'][e], p['w_uq'][e], p['kv_norm'][e], p['w_ukv'][e],
                                         p['w_out_e'][e])
            sc_new.append(hs)
            ckvs.append(ckv)
            kpes.append(kpe)
        else:
            o = l // 2
            y, hc = conformer_mixer(h, h_conf[o], p['w_pw1'][o], p['b_pw1'][o], p['conv_c'][o], p['conv_c_b'][o],
                                    p['ln_c_g'][o], p['ln_c_b'][o], p['w_pw2'][o], p['b_pw2'][o])
            cf_new.append(hc)
        x = x + y
        h = rmsnorm(x, p['norm_ffn'][l])
        y, hf = conv_ffn(h, h_ffn[l], p['w_up'][l], p['conv_f'][l], p['w_down'][l])
        ff_new.append(hf)
        x = x + y
    x = rmsnorm(x, p['norm_final'])
    return (x, jnp.stack(ckvs), jnp.stack(kpes), jnp.stack(sc_new), jnp.stack(cf_new), jnp.stack(ff_new))


def setup_inputs(seed: int = 0) -> dict:
    key = jax.random.key(seed)
    ks = jax.random.split(key, 32)
    f32 = jnp.float32
    n_pages = PAST_LEN // PAGE_SIZE
    n_used = DEC_BATCH * n_pages
    n_pool = (5 * n_used) // 4
    nrm = lambda k, shape, s: jax.random.normal(k, shape, f32) * s
    page_table = jax.random.permutation(ks[4], n_pool)[:n_used].reshape(DEC_BATCH, n_pages).astype(jnp.int32)
    return {
        "x_prompt": nrm(ks[0], (BATCH, SEQ, D_MODEL), 1.0),
        "x_sample": nrm(ks[1], (DEC_BATCH, DEC_SEQ, D_MODEL), 1.0),
        "cache_ckv": nrm(ks[2], (N_EVEN, n_pool, PAGE_SIZE, KV_LORA), 1.0),
        "cache_kpe": nrm(ks[3], (N_EVEN, n_pool, PAGE_SIZE, QK_ROPE), 1.0),
        "page_table": page_table,
        "state_sconv": nrm(ks[5], (N_EVEN, DEC_BATCH, A_WIDTH - 1, D_A), 1.0),
        "state_conf": nrm(ks[6], (N_ODD, DEC_BATCH, CONF_WIDTH - 1, D_CONF), 1.0),
        "state_ffn": nrm(ks[7], (DEPTH, DEC_BATCH, FF_WIDTH - 1, 2 * D_FF), 1.0),
        "norm_mix": 1.0 + nrm(ks[8], (DEPTH, D_MODEL), 0.02),
        "norm_ffn": 1.0 + nrm(ks[9], (DEPTH, D_MODEL), 0.02),
        "norm_final": 1.0 + nrm(ks[10], (D_MODEL,), 0.02),
        "w_in_e": nrm(ks[11], (N_EVEN, D_MODEL, IN_E), D_MODEL ** -0.5),
        "conv_a": nrm(ks[12], (N_EVEN, A_WIDTH, D_A), A_WIDTH ** -0.5),
        "q_norm": 1.0 + nrm(ks[13], (N_EVEN, Q_LORA), 0.02),
        "w_uq": nrm(ks[14], (N_EVEN, Q_LORA, N_HEADS * (QK_NOPE + QK_ROPE)), Q_LORA ** -0.5),
        "kv_norm": 1.0 + nrm(ks[15], (N_EVEN, KV_LORA), 0.02),
        "w_ukv": nrm(ks[16], (N_EVEN, KV_LORA, N_HEADS, QK_NOPE + V_DIM), KV_LORA ** -0.5),
        "w_out_e": nrm(ks[17], (N_EVEN, MIX_E, D_MODEL), MIX_E ** -0.5),
        "w_pw1": nrm(ks[18], (N_ODD, D_MODEL, 2 * D_CONF), D_MODEL ** -0.5),
        "b_pw1": nrm(ks[19], (N_ODD, 2 * D_CONF), 0.01),
        "conv_c": nrm(ks[20], (N_ODD, CONF_WIDTH, D_CONF), CONF_WIDTH ** -0.5),
        "conv_c_b": nrm(ks[21], (N_ODD, D_CONF), 0.01),
        "ln_c_g": 1.0 + nrm(ks[22], (N_ODD, D_CONF), 0.02),
        "ln_c_b": nrm(ks[23], (N_ODD, D_CONF), 0.01),
        "w_pw2": nrm(ks[24], (N_ODD, D_CONF, D_MODEL), D_CONF ** -0.5),
        "b_pw2": nrm(ks[25], (N_ODD, D_MODEL), 0.01),
        "w_up": nrm(ks[26], (DEPTH, D_MODEL, 2 * D_FF), D_MODEL ** -0.5),
        "conv_f": nrm(ks[27], (DEPTH, FF_WIDTH, 2 * D_FF), FF_WIDTH ** -0.5),
        "w_down": nrm(ks[28], (DEPTH, D_FF, D_MODEL), D_FF ** -0.5),
    }


def reference(x_prompt, x_sample, cache_ckv, cache_kpe, page_table, state_sconv, state_conf, state_ffn,
              norm_mix, norm_ffn, norm_final, w_in_e, conv_a, q_norm, w_uq, kv_norm, w_ukv, w_out_e,
              w_pw1, b_pw1, conv_c, conv_c_b, ln_c_g, ln_c_b, w_pw2, b_pw2, w_up, conv_f, w_down):
    p = dict(norm_mix=norm_mix, norm_ffn=norm_ffn, norm_final=norm_final, w_in_e=w_in_e, conv_a=conv_a,
             q_norm=q_norm, w_uq=w_uq, kv_norm=kv_norm, w_ukv=w_ukv, w_out_e=w_out_e, w_pw1=w_pw1,
             b_pw1=b_pw1, conv_c=conv_c, conv_c_b=conv_c_b, ln_c_g=ln_c_g, ln_c_b=ln_c_b, w_pw2=w_pw2,
             b_pw2=b_pw2, w_up=w_up, conv_f=conv_f, w_down=w_down)
    dt = x_prompt.dtype
    bp, s_len = x_prompt.shape[0], x_prompt.shape[1]
    t_len = x_sample.shape[1]
    past_len = page_table.shape[1] * cache_ckv.shape[2]

    z_sc = jnp.zeros((N_EVEN, bp, A_WIDTH - 1, D_A), dt)
    z_cf = jnp.zeros((N_ODD, bp, CONF_WIDTH - 1, D_CONF), dt)
    z_ff = jnp.zeros((DEPTH, bp, FF_WIDTH - 1, 2 * D_FF), dt)
    pos_p = jnp.arange(s_len)
    y_prompt, ckv_p, kpe_p, sc_p, cf_p, ff_p = trunk(
        x_prompt, pos_p, z_sc, z_cf, z_ff, [attn_prompt] * N_EVEN, p)

    pos_s = past_len + jnp.arange(t_len)
    attn_s = [functools.partial(attn_sample, cache_ckv=cache_ckv, cache_kpe=cache_kpe,
                                page_table=page_table, layer=e) for e in range(N_EVEN)]
    y_sample, ckv_s, kpe_s, sc_s, cf_s, ff_s = trunk(
        x_sample, pos_s, state_sconv, state_conf, state_ffn, attn_s, p)

    return (y_prompt, y_sample, ckv_p, kpe_p, ckv_s, kpe_s, sc_p, sc_s, cf_p, cf_s, ff_p, ff_s)
```

```python
import functools
import math

import jax
import jax.numpy as jnp
from jax import lax
from jax.experimental import pallas as pl
from jax.experimental.pallas import tpu as pltpu

F32 = jnp.float32
BF16 = jnp.bfloat16
EPS = 1e-6
ROPE_THETA = 10000.0

V7X_LANES = 128
V7X_SUBLANES = 8
V7X_VMEM_BYTES = 64 * 1024 * 1024
VMEM_LIMIT_BYTES = V7X_VMEM_BYTES - 6 * 1024 * 1024

TOKEN_TILE = 512
FF_TILE = 256
CH_TILE = 256
K_TILE = 512
ATTN_Q_TILE = 256
ATTN_K_TILE = 256
PROMPT_HIST = V7X_SUBLANES
SAMPLE_KEYS_PER_STEP = 4096
ACC_COLS = 512
NORM_ROWS = 128


def _cparams(sem):
    return pltpu.CompilerParams(dimension_semantics=sem, vmem_limit_bytes=VMEM_LIMIT_BYTES)


def _sigmoid(x):
    return 1.0 / (1.0 + jnp.exp(-x))


def _dot(a, b):
    return jnp.dot(a, b, preferred_element_type=F32)


def _dot_nt(a, b):
    return lax.dot_general(a, b, (((1,), (1,)), ((), ())), preferred_element_type=F32)


def _rms_rows(x, gain):
    return x * lax.rsqrt(jnp.mean(x * x, axis=-1, keepdims=True) + EPS) * gain


def _accumulate_dot(acc_ref, a, w_ref, bias_ref=None):
    d = acc_ref.shape[1]
    step = min(d, ACC_COLS)
    for n0 in range(0, d, step):
        cols = slice(n0, n0 + step)
        part = _dot(a, w_ref[:, cols])
        if bias_ref is not None:
            part = part + bias_ref[:, cols]
        acc_ref[:, cols] += part


def _store_rms(dst_ref, src_ref, gain_ref):
    tm = src_ref.shape[0]
    step = min(tm, NORM_ROWS)
    for r0 in range(0, tm, step):
        rows = pl.ds(r0, step)
        dst_ref[rows, :] = _rms_rows(src_ref[rows, :], gain_ref[...]).astype(dst_ref.dtype)


def _causal_taps(full_ref, w_ref, hist_rows, tm, shift):
    kw = w_ref.shape[0]
    acc = None
    for k in range(kw):
        off = hist_rows - (kw - 1 - k) * shift
        term = full_ref[pl.ds(off, tm), :] * w_ref[pl.ds(k, 1), :]
        acc = term if acc is None else acc + term
    return acc


def _stage_history(full_ref, hist_ref, carry_ref, tail_ref, slot, first_of_seq, hist_rows, tm):
    if hist_ref is not None:
        full_ref[pl.ds(0, hist_rows), :] = hist_ref[...]
        tail_rows = tail_ref.shape[-2]
        tail_ref[...] = full_ref[pl.ds(hist_rows + tm - tail_rows, tail_rows), :]
        return

    @pl.when(first_of_seq)
    def _():
        full_ref[pl.ds(0, hist_rows), :] = jnp.zeros((hist_rows, full_ref.shape[1]), F32)

    @pl.when(jnp.logical_not(first_of_seq))
    def _():
        full_ref[pl.ds(0, hist_rows), :] = carry_ref[slot]

    tail = full_ref[pl.ds(tm, hist_rows), :]
    carry_ref[slot] = tail
    tail_ref[...] = tail


def _rms_body(x_ref, g_ref, o_ref):
    _store_rms(o_ref, x_ref, g_ref)


def _rmsnorm(x, gain, tm):
    n, d = x.shape
    return pl.pallas_call(
        _rms_body,
        grid=(n // tm,),
        in_specs=[pl.BlockSpec((tm, d), lambda i: (i, 0)), pl.BlockSpec((1, d), lambda i: (0, 0))],
        out_specs=pl.BlockSpec((tm, d), lambda i: (i, 0)),
        out_shape=jax.ShapeDtypeStruct((n, d), BF16),
        compiler_params=_cparams(("arbitrary",)),
        name="rmsnorm",
    )(x, gain.reshape(1, d))


def _ln_silu_body(c_ref, g_ref, b_ref, o_ref):
    tm = c_ref.shape[0]
    step = min(tm, NORM_ROWS)
    for r0 in range(0, tm, step):
        rows = pl.ds(r0, step)
        c = c_ref[rows, :]
        mu = jnp.mean(c, axis=-1, keepdims=True)
        xc = c - mu
        y = xc * lax.rsqrt(jnp.mean(xc * xc, axis=-1, keepdims=True) + EPS)
        y = y * g_ref[...] + b_ref[...]
        o_ref[rows, :] = (y * _sigmoid(y)).astype(o_ref.dtype)


def _ln_silu(c, gain, bias, tm):
    n, d = c.shape
    return pl.pallas_call(
        _ln_silu_body,
        grid=(n // tm,),
        in_specs=[pl.BlockSpec((tm, d), lambda i: (i, 0)),
                  pl.BlockSpec((1, d), lambda i: (0, 0)),
                  pl.BlockSpec((1, d), lambda i: (0, 0))],
        out_specs=pl.BlockSpec((tm, d), lambda i: (i, 0)),
        out_shape=jax.ShapeDtypeStruct((n, d), BF16),
        compiler_params=_cparams(("arbitrary",)),
        name="ln_silu",
    )(c, gain.reshape(1, d), bias.reshape(1, d))


def _lrn_body(*refs, n_a, kc, n_k, tm):
    a_refs = refs[:n_a]
    w_ref, b_ref, x_hbm, gn_ref, xo_ref, hn_ref, sem = refs[n_a:]
    i = pl.program_id(0)
    k = pl.program_id(1)

    def x_copy():
        return pltpu.make_async_copy(x_hbm.at[pl.ds(i * tm, tm), :], xo_ref, sem)

    @pl.when(k == 0)
    def _():
        x_copy().start()

    for p in range(n_a):
        @pl.when(jnp.logical_and(k >= p * kc, k < (p + 1) * kc))
        def _(p=p):
            a = a_refs[p][...]

            @pl.when(k == 0)
            def _():
                x_copy().wait()
                _accumulate_dot(xo_ref, a, w_ref, b_ref)

            @pl.when(k != 0)
            def _():
                _accumulate_dot(xo_ref, a, w_ref)

    @pl.when(k == n_k - 1)
    def _():
        _store_rms(hn_ref, xo_ref, gn_ref)


def _linear_residual_norm(a_list, w, bias, x, gain, tm, tk, out_dtype):
    n, d = x.shape
    n_a = len(a_list)
    k_each = a_list[0].shape[1]
    kc = k_each // tk
    n_k = n_a * kc
    a_specs = [
        pl.BlockSpec((tm, tk), lambda i, k, p=p: (i, jnp.clip(k - p * kc, 0, kc - 1)))
        for p in range(n_a)
    ]
    return pl.pallas_call(
        functools.partial(_lrn_body, n_a=n_a, kc=kc, n_k=n_k, tm=tm),
        grid=(n // tm, n_k),
        in_specs=a_specs + [
            pl.BlockSpec((tk, d), lambda i, k: (k, 0)),
            pl.BlockSpec((1, d), lambda i, k: (0, 0)),
            pl.BlockSpec(memory_space=pl.ANY),
            pl.BlockSpec((1, d), lambda i, k: (0, 0)),
        ],
        out_specs=[pl.BlockSpec((tm, d), lambda i, k: (i, 0)),
                   pl.BlockSpec((tm, d), lambda i, k: (i, 0))],
        out_shape=[jax.ShapeDtypeStruct((n, d), F32), jax.ShapeDtypeStruct((n, d), out_dtype)],
        scratch_shapes=[pltpu.SemaphoreType.DMA(())],
        compiler_params=_cparams(("arbitrary", "arbitrary")),
        name="linear_residual_norm",
    )(*a_list, w, bias.reshape(1, d), x, gain.reshape(1, d))


def _ffn_body(*refs, tm, n_j, hist_rows, shift, tiles_per_seq, sample):
    if sample:
        (h_ref, x_hbm, wg_ref, wu_ref, cg_ref, cu_ref, wd_ref, gn_ref, hg_ref, hu_ref,
         xo_ref, hn_ref, tg_ref, tu_ref, fg_ref, fu_ref, sem) = refs
        cag_ref = cau_ref = None
    else:
        (h_ref, x_hbm, wg_ref, wu_ref, cg_ref, cu_ref, wd_ref, gn_ref,
         xo_ref, hn_ref, tg_ref, tu_ref, fg_ref, fu_ref, cag_ref, cau_ref, sem) = refs
        hg_ref = hu_ref = None
    i = pl.program_id(0)
    j = pl.program_id(1)

    def x_copy():
        return pltpu.make_async_copy(x_hbm.at[pl.ds(i * tm, tm), :], xo_ref, sem)

    @pl.when(j == 0)
    def _():
        x_copy().start()

    h = h_ref[...]
    fg_ref[pl.ds(hist_rows, tm), :] = _dot(h, wg_ref[...])
    fu_ref[pl.ds(hist_rows, tm), :] = _dot(h, wu_ref[...])
    first = (i % tiles_per_seq) == 0
    _stage_history(fg_ref, hg_ref, cag_ref, tg_ref, j, first, hist_rows, tm)
    _stage_history(fu_ref, hu_ref, cau_ref, tu_ref, j, first, hist_rows, tm)
    cg = _causal_taps(fg_ref, cg_ref, hist_rows, tm, shift)
    cu = _causal_taps(fu_ref, cu_ref, hist_rows, tm, shift)
    act = (cg * _sigmoid(cg) * cu).astype(BF16)

    @pl.when(j == 0)
    def _():
        x_copy().wait()

    _accumulate_dot(xo_ref, act, wd_ref)

    @pl.when(j == n_j - 1)
    def _():
        _store_rms(hn_ref, xo_ref, gn_ref)


def _conv_ffn(h, x, w_up, conv_w, w_down, gain, hist, *, tm, tf, shift, tiles_per_seq, out_dtype):
    n, d = x.shape
    d_ff = w_down.shape[0]
    n_j = d_ff // tf
    n_i = n // tm
    sample = hist is not None
    hist_rows = 2 * shift if sample else PROMPT_HIST
    in_specs = [
        pl.BlockSpec((tm, d), lambda i, j: (i, 0)),
        pl.BlockSpec(memory_space=pl.ANY),
        pl.BlockSpec((d, tf), lambda i, j: (0, j)),
        pl.BlockSpec((d, tf), lambda i, j: (0, n_j + j)),
        pl.BlockSpec((3, tf), lambda i, j: (0, j)),
        pl.BlockSpec((3, tf), lambda i, j: (0, n_j + j)),
        pl.BlockSpec((tf, d), lambda i, j: (j, 0)),
        pl.BlockSpec((1, d), lambda i, j: (0, 0)),
    ]
    args = [h, x, w_up, w_up, conv_w, conv_w, w_down, gain.reshape(1, d)]
    scratch = [pltpu.VMEM((hist_rows + tm, tf), F32), pltpu.VMEM((hist_rows + tm, tf), F32)]
    if sample:
        in_specs += [pl.BlockSpec((hist_rows, tf), lambda i, j: (0, j)),
                     pl.BlockSpec((hist_rows, tf), lambda i, j: (0, n_j + j))]
        args += [hist, hist]
        tail_shape = jax.ShapeDtypeStruct((hist_rows, d_ff), F32)
        tail_spec = pl.BlockSpec((hist_rows, tf), lambda i, j: (0, j))
    else:
        scratch += [pltpu.VMEM((n_j, hist_rows, tf), F32), pltpu.VMEM((n_j, hist_rows, tf), F32)]
        tail_shape = jax.ShapeDtypeStruct((n_i, hist_rows, d_ff), F32)
        tail_spec = pl.BlockSpec((None, hist_rows, tf), lambda i, j: (i, 0, j))
    scratch.append(pltpu.SemaphoreType.DMA(()))
    return pl.pallas_call(
        functools.partial(_ffn_body, tm=tm, n_j=n_j, hist_rows=hist_rows, shift=shift,
                          tiles_per_seq=tiles_per_seq, sample=sample),
        grid=(n_i, n_j),
        in_specs=in_specs,
        out_specs=[pl.BlockSpec((tm, d), lambda i, j: (i, 0)),
                   pl.BlockSpec((tm, d), lambda i, j: (i, 0)),
                   tail_spec, tail_spec],
        out_shape=[jax.ShapeDtypeStruct((n, d), F32), jax.ShapeDtypeStruct((n, d), out_dtype),
                   tail_shape, tail_shape],
        scratch_shapes=scratch,
        compiler_params=_cparams(("arbitrary", "arbitrary")),
        name="conv_ffn",
    )(*args)


def _sconv_body(*refs, tm, hist_rows, shift, tiles_per_seq, sample):
    if sample:
        h_ref, wb_ref, wc_ref, wx_ref, cw_ref, hist_ref, ya_ref, tail_ref, full_ref = refs
        carry_ref = None
    else:
        h_ref, wb_ref, wc_ref, wx_ref, cw_ref, ya_ref, tail_ref, full_ref, carry_ref = refs
        hist_ref = None
    i = pl.program_id(0)
    c = pl.program_id(1)
    h = h_ref[...]
    full_ref[pl.ds(hist_rows, tm), :] = _dot(h, wc_ref[...]) * _dot(h, wx_ref[...])
    first = (i % tiles_per_seq) == 0
    _stage_history(full_ref, hist_ref, carry_ref, tail_ref, c, first, hist_rows, tm)
    conv = _causal_taps(full_ref, cw_ref, hist_rows, tm, shift)
    ya_ref[...] = (_dot(h, wb_ref[...]) * conv).astype(ya_ref.dtype)


def _short_conv(h, w_in, conv_w, hist, *, d_a, tm, tc, shift, tiles_per_seq):
    n, d = h.shape
    n_c = d_a // tc
    n_i = n // tm
    sample = hist is not None
    hist_rows = 2 * shift if sample else PROMPT_HIST
    in_specs = [
        pl.BlockSpec((tm, d), lambda i, c: (i, 0)),
        pl.BlockSpec((d, tc), lambda i, c: (0, c)),
        pl.BlockSpec((d, tc), lambda i, c: (0, n_c + c)),
        pl.BlockSpec((d, tc), lambda i, c: (0, 2 * n_c + c)),
        pl.BlockSpec((3, tc), lambda i, c: (0, c)),
    ]
    args = [h, w_in, w_in, w_in, conv_w]
    scratch = [pltpu.VMEM((hist_rows + tm, tc), F32)]
    if sample:
        in_specs.append(pl.BlockSpec((hist_rows, tc), lambda i, c: (0, c)))
        args.append(hist)
        tail_shape = jax.ShapeDtypeStruct((hist_rows, d_a), F32)
        tail_spec = pl.BlockSpec((hist_rows, tc), lambda i, c: (0, c))
    else:
        scratch.append(pltpu.VMEM((n_c, hist_rows, tc), F32))
        tail_shape = jax.ShapeDtypeStruct((n_i, hist_rows, d_a), F32)
        tail_spec = pl.BlockSpec((None, hist_rows, tc), lambda i, c: (i, 0, c))
    return pl.pallas_call(
        functools.partial(_sconv_body, tm=tm, hist_rows=hist_rows, shift=shift,
                          tiles_per_seq=tiles_per_seq, sample=sample),
        grid=(n_i, n_c),
        in_specs=in_specs,
        out_specs=[pl.BlockSpec((tm, tc), lambda i, c: (i, c)), tail_spec],
        out_shape=[jax.ShapeDtypeStruct((n, d_a), BF16), tail_shape],
        scratch_shapes=scratch,
        compiler_params=_cparams(("arbitrary", "arbitrary")),
        name="short_conv",
    )(*args)


def _glu_conv_body(*refs, tm, hist_rows, shift, tiles_per_seq, sample):
    if sample:
        (h_ref, wlo_ref, whi_ref, blo_ref, bhi_ref, cw_ref, cb_ref, hist_ref,
         c_ref, tail_ref, full_ref) = refs
        carry_ref = None
    else:
        (h_ref, wlo_ref, whi_ref, blo_ref, bhi_ref, cw_ref, cb_ref,
         c_ref, tail_ref, full_ref, carry_ref) = refs
        hist_ref = None
    i = pl.program_id(0)
    c = pl.program_id(1)
    h = h_ref[...]
    a_lo = _dot(h, wlo_ref[...]) + blo_ref[...]
    a_hi = _dot(h, whi_ref[...]) + bhi_ref[...]
    full_ref[pl.ds(hist_rows, tm), :] = a_lo * _sigmoid(a_hi)
    first = (i % tiles_per_seq) == 0
    _stage_history(full_ref, hist_ref, carry_ref, tail_ref, c, first, hist_rows, tm)
    c_ref[...] = _causal_taps(full_ref, cw_ref, hist_rows, tm, shift) + cb_ref[...]


def _glu_conv(h, w1, b1, conv_w, conv_b, hist, *, tm, tc, shift, tiles_per_seq):
    n, d = h.shape
    d_c = conv_w.shape[1]
    kw = conv_w.shape[0]
    n_c = d_c // tc
    n_i = n // tm
    sample = hist is not None
    hist_rows = (kw - 1) * shift if sample else -(-(kw - 1) // V7X_SUBLANES) * V7X_SUBLANES
    in_specs = [
        pl.BlockSpec((tm, d), lambda i, c: (i, 0)),
        pl.BlockSpec((d, tc), lambda i, c: (0, c)),
        pl.BlockSpec((d, tc), lambda i, c: (0, n_c + c)),
        pl.BlockSpec((1, tc), lambda i, c: (0, c)),
        pl.BlockSpec((1, tc), lambda i, c: (0, n_c + c)),
        pl.BlockSpec((kw, tc), lambda i, c: (0, c)),
        pl.BlockSpec((1, tc), lambda i, c: (0, c)),
    ]
    b1r = b1.reshape(1, 2 * d_c)
    args = [h, w1, w1, b1r, b1r, conv_w, conv_b.reshape(1, d_c)]
    scratch = [pltpu.VMEM((hist_rows + tm, tc), F32)]
    if sample:
        in_specs.append(pl.BlockSpec((hist_rows, tc), lambda i, c: (0, c)))
        args.append(hist)
        tail_shape = jax.ShapeDtypeStruct((hist_rows, d_c), F32)
        tail_spec = pl.BlockSpec((hist_rows, tc), lambda i, c: (0, c))
    else:
        scratch.append(pltpu.VMEM((n_c, hist_rows, tc), F32))
        tail_shape = jax.ShapeDtypeStruct((n_i, hist_rows, d_c), F32)
        tail_spec = pl.BlockSpec((None, hist_rows, tc), lambda i, c: (i, 0, c))
    return pl.pallas_call(
        functools.partial(_glu_conv_body, tm=tm, hist_rows=hist_rows, shift=shift,
                          tiles_per_seq=tiles_per_seq, sample=sample),
        grid=(n_i, n_c),
        in_specs=in_specs,
        out_specs=[pl.BlockSpec((tm, tc), lambda i, c: (i, c)), tail_spec],
        out_shape=[jax.ShapeDtypeStruct((n, d_c), F32), tail_shape],
        scratch_shapes=scratch,
        compiler_params=_cparams(("arbitrary", "arbitrary")),
        name="glu_conv",
    )(*args)


def _qkv_body(h_ref, w_ref, qg_ref, kg_ref, cos_ref, sin_ref,
              qn_ref, ckv_ref, kpe_ref, kf_ref, *, q_lora, kv_lora):
    z = _dot(h_ref[...], w_ref[...])
    qn_ref[...] = _rms_rows(z[:, :q_lora], qg_ref[...]).astype(qn_ref.dtype)
    ckv = _rms_rows(z[:, q_lora:q_lora + kv_lora], kg_ref[...])
    o = q_lora + kv_lora
    kpe = (z[:, o:o + V7X_LANES] * cos_ref[...]
           + z[:, o + V7X_LANES:o + 2 * V7X_LANES] * sin_ref[...])
    ckv_ref[...] = ckv
    kpe_ref[...] = kpe
    kf_ref[:, :kv_lora] = ckv.astype(kf_ref.dtype)
    kf_ref[:, kv_lora:] = kpe.astype(kf_ref.dtype)


def _qkv_proj(h, w_qkv, q_gain, kv_gain, cos_t, sin_t, *, tm, rope_tiles):
    n, d = h.shape
    q_lora = q_gain.shape[0]
    kv_lora = kv_gain.shape[0]
    wn = w_qkv.shape[1]
    kfw = kv_lora + V7X_LANES
    return pl.pallas_call(
        functools.partial(_qkv_body, q_lora=q_lora, kv_lora=kv_lora),
        grid=(n // tm,),
        in_specs=[
            pl.BlockSpec((tm, d), lambda i: (i, 0)),
            pl.BlockSpec((d, wn), lambda i: (0, 0)),
            pl.BlockSpec((1, q_lora), lambda i: (0, 0)),
            pl.BlockSpec((1, kv_lora), lambda i: (0, 0)),
            pl.BlockSpec((tm, V7X_LANES), lambda i: (i % rope_tiles, 0)),
            pl.BlockSpec((tm, V7X_LANES), lambda i: (i % rope_tiles, 0)),
        ],
        out_specs=[
            pl.BlockSpec((tm, q_lora), lambda i: (i, 0)),
            pl.BlockSpec((tm, kv_lora), lambda i: (i, 0)),
            pl.BlockSpec((tm, V7X_LANES), lambda i: (i, 0)),
            pl.BlockSpec((tm, kfw), lambda i: (i, 0)),
        ],
        out_shape=[
            jax.ShapeDtypeStruct((n, q_lora), BF16),
            jax.ShapeDtypeStruct((n, kv_lora), F32),
            jax.ShapeDtypeStruct((n, V7X_LANES), F32),
            jax.ShapeDtypeStruct((n, kfw), BF16),
        ],
        compiler_params=_cparams(("arbitrary",)),
        name="qkv_proj",
    )(h, w_qkv, q_gain.reshape(1, q_lora), kv_gain.reshape(1, kv_lora), cos_t, sin_t)


def _q_up_body(qn_ref, wn_ref, wp_ref, wr_ref, uk_ref, cos_ref, sin_ref, o_ref, *, hp, nope, kv_lora):
    qn = qn_ref[...]
    q_nope = _dot(qn, wn_ref[...]).astype(BF16)
    pe = _dot(qn, wp_ref[...])
    rot = _dot(qn, wr_ref[...])
    cos = cos_ref[...]
    sin = sin_ref[...]
    for hh in range(hp):
        lat = _dot(q_nope[:, hh * nope:(hh + 1) * nope], uk_ref[hh])
        sl = slice(hh * V7X_LANES, (hh + 1) * V7X_LANES)
        o_ref[hh, :, :kv_lora] = lat.astype(o_ref.dtype)
        o_ref[hh, :, kv_lora:] = (pe[:, sl] * cos + rot[:, sl] * sin).astype(o_ref.dtype)


def _q_up(qn, w_nope, w_pe, w_rot, w_ukt, cos_t, sin_t, *, tm, rope_tiles, hp):
    n, q_lora = qn.shape
    n_heads, nope, kv_lora = w_ukt.shape
    qw = kv_lora + V7X_LANES
    return pl.pallas_call(
        functools.partial(_q_up_body, hp=hp, nope=nope, kv_lora=kv_lora),
        grid=(n // tm, n_heads // hp),
        in_specs=[
            pl.BlockSpec((tm, q_lora), lambda i, g: (i, 0)),
            pl.BlockSpec((q_lora, hp * nope), lambda i, g: (0, g)),
            pl.BlockSpec((q_lora, hp * V7X_LANES), lambda i, g: (0, g)),
            pl.BlockSpec((q_lora, hp * V7X_LANES), lambda i, g: (0, g)),
            pl.BlockSpec((hp, nope, kv_lora), lambda i, g: (g, 0, 0)),
            pl.BlockSpec((tm, V7X_LANES), lambda i, g: (i % rope_tiles, 0)),
            pl.BlockSpec((tm, V7X_LANES), lambda i, g: (i % rope_tiles, 0)),
        ],
        out_specs=pl.BlockSpec((hp, tm, qw), lambda i, g: (g, i, 0)),
        out_shape=jax.ShapeDtypeStruct((n_heads, n, qw), BF16),
        compiler_params=_cparams(("arbitrary", "arbitrary")),
        name="q_up",
    )(qn, w_nope, w_pe, w_rot, w_ukt, cos_t, sin_t)


def _softmax_update(s, v, m_ref, l_ref, acc_ref):
    m_prev = m_ref[...]
    m_new = jnp.maximum(m_prev, jnp.max(s, axis=-1, keepdims=True))
    alpha = jnp.exp(m_prev - m_new)
    p = jnp.exp(s - m_new)
    l_ref[...] = alpha * l_ref[...] + jnp.sum(p, axis=-1, keepdims=True)
    acc_ref[...] = alpha * acc_ref[...] + _dot(p.astype(BF16), v)
    m_ref[...] = m_new


def _attn_prompt_body(q_ref, k_ref, wv_ref, o_ref, m_ref, l_ref, acc_ref,
                      *, tq, tk, n_k, n_heads, kv_lora, v_dim, scale):
    qi = pl.program_id(1)
    ki = pl.program_id(2)
    last = ((qi + 1) * tq - 1) // tk
    rows = n_heads * tq

    @pl.when(ki == 0)
    def _():
        m_ref[...] = jnp.full(m_ref.shape, -jnp.inf, F32)
        l_ref[...] = jnp.zeros(l_ref.shape, F32)
        acc_ref[...] = jnp.zeros(acc_ref.shape, F32)

    @pl.when(ki <= last)
    def _():
        q = q_ref[...].reshape(rows, q_ref.shape[-1])
        k = k_ref[...]
        s = _dot_nt(q, k) * scale
        q_pos = qi * tq + lax.broadcasted_iota(jnp.int32, (n_heads, tq, tk), 1).reshape(rows, tk)
        k_pos = ki * tk + lax.broadcasted_iota(jnp.int32, (rows, tk), 1)
        s = jnp.where(k_pos <= q_pos, s, -jnp.inf)
        _softmax_update(s, k[:, :kv_lora], m_ref, l_ref, acc_ref)

    @pl.when(ki == n_k - 1)
    def _():
        o_lat = (acc_ref[...] / l_ref[...]).astype(BF16)
        for hh in range(n_heads):
            o_ref[:, hh * v_dim:(hh + 1) * v_dim] = _dot(
                o_lat[hh * tq:(hh + 1) * tq], wv_ref[hh]).astype(o_ref.dtype)


def _attn_prompt(q, kf, w_uvh, *, batch, seq, tq, tk, scale):
    n_heads, n, qw = q.shape
    _, kv_lora, v_dim = w_uvh.shape
    n_q = seq // tq
    n_k = seq // tk
    rows = n_heads * tq

    def k_map(b, qi, ki):
        last = ((qi + 1) * tq - 1) // tk
        return (b * n_k + jnp.minimum(ki, last), 0)

    return pl.pallas_call(
        functools.partial(_attn_prompt_body, tq=tq, tk=tk, n_k=n_k, n_heads=n_heads,
                          kv_lora=kv_lora, v_dim=v_dim, scale=scale),
        grid=(batch, n_q, n_k),
        in_specs=[
            pl.BlockSpec((n_heads, tq, qw), lambda b, qi, ki: (0, b * n_q + qi, 0)),
            pl.BlockSpec((tk, qw), k_map),
            pl.BlockSpec((n_heads, kv_lora, v_dim), lambda b, qi, ki: (0, 0, 0)),
        ],
        out_specs=pl.BlockSpec((tq, n_heads * v_dim), lambda b, qi, ki: (b * n_q + qi, 0)),
        out_shape=jax.ShapeDtypeStruct((n, n_heads * v_dim), BF16),
        scratch_shapes=[pltpu.VMEM((rows, 1), F32), pltpu.VMEM((rows, 1), F32),
                        pltpu.VMEM((rows, kv_lora), F32)],
        compiler_params=_cparams(("arbitrary", "arbitrary", "arbitrary")),
        name="attn_prompt",
    )(q, kf, w_uvh)


def _attn_sample_body(pt_ref, q_ref, kn_ref, ckv_hbm, kpe_hbm, o_ref,
                      cbuf, pbuf, sem, m_ref, l_ref, acc_ref,
                      *, layer, pages, page, n_half, n_heads, n_new, kv_lora, rope, scale):
    b = pl.program_id(0)
    hf = pl.program_id(1)
    step = b * n_half + hf
    n_steps = pl.num_programs(0) * n_half
    slot = step % 2

    def page_copies(bb, hh, sl):
        cps = []
        for p in range(pages):
            pg = pt_ref[bb, hh * pages + p]
            rows = pl.ds(p * page, page)
            cps.append(pltpu.make_async_copy(ckv_hbm.at[layer, pg], cbuf.at[sl, rows, :], sem.at[0, sl]))
            cps.append(pltpu.make_async_copy(kpe_hbm.at[layer, pg], pbuf.at[sl, rows, :], sem.at[1, sl]))
        return cps

    @pl.when(step == 0)
    def _():
        for cp in page_copies(b, hf, slot):
            cp.start()

    @pl.when(step + 1 < n_steps)
    def _():
        wrap = hf == n_half - 1
        nb = jnp.where(wrap, b + 1, b)
        nh = jnp.where(wrap, 0, hf + 1)
        for cp in page_copies(nb, nh, 1 - slot):
            cp.start()

    for cp in page_copies(b, hf, slot):
        cp.wait()

    @pl.when(hf == 0)
    def _():
        m_ref[...] = jnp.full(m_ref.shape, -jnp.inf, F32)
        l_ref[...] = jnp.zeros(l_ref.shape, F32)
        acc_ref[...] = jnp.zeros(acc_ref.shape, F32)

    q = q_ref[0]
    kc = cbuf[slot].astype(BF16)
    kp = pbuf[slot].astype(BF16)
    s = (_dot_nt(q[:, :kv_lora], kc) + _dot_nt(q[:, kv_lora:kv_lora + rope], kp)) * scale
    _softmax_update(s, kc, m_ref, l_ref, acc_ref)

    @pl.when(hf == n_half - 1)
    def _():
        kn = kn_ref[0]
        sn = _dot_nt(q, kn) * scale
        t_row = lax.broadcasted_iota(jnp.int32, sn.shape, 0) // n_heads
        u_col = lax.broadcasted_iota(jnp.int32, sn.shape, 1)
        sn = jnp.where(jnp.logical_and(u_col <= t_row, u_col < n_new), sn, -jnp.inf)
        _softmax_update(sn, kn[:, :kv_lora], m_ref, l_ref, acc_ref)
        o_ref[0] = (acc_ref[...] / l_ref[...]).astype(o_ref.dtype)


def _attn_sample(page_table, q, k_new, cache_ckv, cache_kpe, *, layer, n_new, n_heads, scale):
    bd, rows, qw = q.shape
    _, _, page, kv_lora = cache_ckv.shape
    rope = cache_kpe.shape[-1]
    n_pages = page_table.shape[1]
    past = n_pages * page
    keys = min(SAMPLE_KEYS_PER_STEP, past)
    n_half = past // keys
    pages = keys // page
    grid_spec = pltpu.PrefetchScalarGridSpec(
        num_scalar_prefetch=1,
        grid=(bd, n_half),
        in_specs=[
            pl.BlockSpec((1, rows, qw), lambda b, hf, pt: (b, 0, 0)),
            pl.BlockSpec((1, k_new.shape[1], qw), lambda b, hf, pt: (b, 0, 0)),
            pl.BlockSpec(memory_space=pl.ANY),
            pl.BlockSpec(memory_space=pl.ANY),
        ],
        out_specs=pl.BlockSpec((1, rows, kv_lora), lambda b, hf, pt: (b, 0, 0)),
        scratch_shapes=[
            pltpu.VMEM((2, keys, kv_lora), F32),
            pltpu.VMEM((2, keys, rope), F32),
            pltpu.SemaphoreType.DMA((2, 2)),
            pltpu.VMEM((rows, 1), F32),
            pltpu.VMEM((rows, 1), F32),
            pltpu.VMEM((rows, kv_lora), F32),
        ],
    )
    return pl.pallas_call(
        functools.partial(_attn_sample_body, layer=layer, pages=pages, page=page, n_half=n_half,
                          n_heads=n_heads, n_new=n_new, kv_lora=kv_lora, rope=rope, scale=scale),
        grid_spec=grid_spec,
        out_shape=jax.ShapeDtypeStruct((bd, rows, kv_lora), BF16),
        compiler_params=_cparams(("arbitrary", "arbitrary")),
        name="attn_sample",
    )(page_table, q, k_new, cache_ckv, cache_kpe)


def _head_proj_body(x_ref, w_ref, o_ref):
    o_ref[...] = _dot(x_ref[0], w_ref[0]).astype(o_ref.dtype)


def _head_proj(o_lat, w_uvh):
    n_heads, n, kv_lora = o_lat.shape
    v_dim = w_uvh.shape[-1]
    return pl.pallas_call(
        _head_proj_body,
        grid=(n_heads,),
        in_specs=[pl.BlockSpec((1, n, kv_lora), lambda h: (h, 0, 0)),
                  pl.BlockSpec((1, kv_lora, v_dim), lambda h: (h, 0, 0))],
        out_specs=pl.BlockSpec((n, v_dim), lambda h: (0, h)),
        out_shape=jax.ShapeDtypeStruct((n, n_heads * v_dim), BF16),
        compiler_params=_cparams(("arbitrary",)),
        name="head_proj",
    )(o_lat, w_uvh)


def _rope_tables(pos, rope):
    half = rope // 2
    inv = 1.0 / (ROPE_THETA ** (jnp.arange(half, dtype=F32) * (2.0 / rope)))
    ang = pos.astype(F32)[:, None] * inv[None, :]
    cos = jnp.cos(ang)
    sin = jnp.sin(ang)
    pad = jnp.zeros((pos.shape[0], V7X_LANES - rope), F32)
    return (jnp.concatenate([cos, cos, pad], axis=-1), jnp.concatenate([-sin, sin, pad], axis=-1))


def _swap_halves(w):
    half = w.shape[-1] // 2
    return jnp.concatenate([w[..., half:], w[..., :half]], axis=-1)


def _pad_lanes(w):
    pad = [(0, 0)] * (w.ndim - 1) + [(0, V7X_LANES - w.shape[-1])]
    return jnp.pad(w, pad)


def _time_major(a):
    b, t, c = a.shape
    return jnp.swapaxes(a, 0, 1).reshape(t * b, c)


def _batch_major(a, b):
    tb, c = a.shape
    return jnp.swapaxes(a.reshape(tb // b, b, c), 0, 1)


def kernel(x_prompt, x_sample, cache_ckv, cache_kpe, page_table, state_sconv, state_conf, state_ffn,
           norm_mix, norm_ffn, norm_final, w_in_e, conv_a, q_norm, w_uq, kv_norm, w_ukv, w_out_e,
           w_pw1, b_pw1, conv_c, conv_c_b, ln_c_g, ln_c_b, w_pw2, b_pw2, w_up, conv_f, w_down):
    bp, seq, d = x_prompt.shape
    bd, t_len, _ = x_sample.shape
    depth = norm_mix.shape[0]
    d_a = conv_a.shape[-1]
    q_lora = q_norm.shape[-1]
    kv_lora = kv_norm.shape[-1]
    rope = cache_kpe.shape[-1]
    n_heads = w_ukv.shape[2]
    nope = w_uq.shape[-1] // n_heads - rope
    v_dim = w_ukv.shape[-1] - nope
    page = cache_ckv.shape[2]
    past_len = page_table.shape[1] * page
    scale = float(nope + rope) ** -0.5
    n_p = bp * seq
    n_s = bd * t_len

    tm_p = min(TOKEN_TILE, seq)
    tps = seq // tm_p
    tq = min(ATTN_Q_TILE, seq)
    tk = min(ATTN_K_TILE, seq)
    tf = min(FF_TILE, w_down.shape[1])
    tc_a = min(CH_TILE, d_a)
    tc_c = min(CH_TILE, d)
    tk_out = min(K_TILE, d_a)

    cos_p, sin_p = _rope_tables(jnp.arange(seq), rope)
    cos_s, sin_s = _rope_tables(past_len + jnp.repeat(jnp.arange(t_len), bd), rope)

    groups = (
        dict(n=n_p, tm=tm_p, shift=1, tps=tps, cos=cos_p, sin=sin_p, rope_tiles=tps),
        dict(n=n_s, tm=n_s, shift=bd, tps=1, cos=cos_s, sin=sin_s, rope_tiles=1),
    )
    xs = [x_prompt.reshape(n_p, d), _time_major(x_sample)]
    hs = [_rmsnorm(x, norm_mix[0], g["tm"]) for x, g in zip(xs, groups)]

    sc_new, cf_new, ff_new, ckvs, kpes = [[], []], [[], []], [[], []], [[], []], [[], []]

    def prompt_tail(tail, rows):
        return tail[tps - 1::tps, tail.shape[1] - rows:, :]

    for l in range(depth):
        nxt_gain = norm_ffn[l]
        if l % 2 == 0:
            e = l // 2
            w_in = w_in_e[e]
            w_in_b = w_in.astype(BF16)
            o = 3 * d_a
            w_kr = w_in[:, o + q_lora + kv_lora:]
            w_qkv = jnp.concatenate(
                [w_in[:, o:o + q_lora + kv_lora], _pad_lanes(w_kr), _pad_lanes(_swap_halves(w_kr))],
                axis=-1).astype(BF16)
            wq = w_uq[e].reshape(q_lora, n_heads, nope + rope)
            w_nope = wq[..., :nope].reshape(q_lora, n_heads * nope).astype(BF16)
            w_pe = _pad_lanes(wq[..., nope:]).reshape(q_lora, n_heads * V7X_LANES).astype(BF16)
            w_rot = _pad_lanes(_swap_halves(wq[..., nope:])).reshape(q_lora, n_heads * V7X_LANES).astype(BF16)
            w_ukt = jnp.transpose(w_ukv[e][..., :nope], (1, 2, 0)).astype(BF16)
            w_uvh = jnp.transpose(w_ukv[e][..., nope:], (1, 0, 2)).astype(BF16)
            w_out = w_out_e[e].astype(BF16)
            for gi, g in enumerate(groups):
                hist = None if gi == 0 else _time_major(state_sconv[e])
                ya, tail = _short_conv(hs[gi], w_in_b, conv_a[e], hist, d_a=d_a, tm=g["tm"], tc=tc_a,
                                       shift=g["shift"], tiles_per_seq=g["tps"])
                qn, ckv, kpe, kf = _qkv_proj(hs[gi], w_qkv, q_norm[e], kv_norm[e], g["cos"], g["sin"],
                                             tm=g["tm"], rope_tiles=g["rope_tiles"])
                qf = _q_up(qn, w_nope, w_pe, w_rot, w_ukt, g["cos"], g["sin"],
                           tm=g["tm"], rope_tiles=g["rope_tiles"], hp=2)
                if gi == 0:
                    o_mix = _attn_prompt(qf, kf, w_uvh, batch=bp, seq=seq, tq=tq, tk=tk, scale=scale)
                    sc_new[gi].append(prompt_tail(tail, 2))
                    ckvs[gi].append(ckv.reshape(bp, seq, kv_lora))
                    kpes[gi].append(kpe[:, :rope].reshape(bp, seq, rope))
                else:
                    qw = qf.shape[-1]
                    q_seq = jnp.transpose(qf.reshape(n_heads, t_len, bd, qw), (2, 1, 0, 3))
                    q_seq = q_seq.reshape(bd, t_len * n_heads, qw)
                    k_new = jnp.pad(_batch_major(kf, bd), ((0, 0), (0, 2 * V7X_SUBLANES - t_len), (0, 0)))
                    o_lat = _attn_sample(page_table, q_seq, k_new, cache_ckv, cache_kpe, layer=e,
                                         n_new=t_len, n_heads=n_heads, scale=scale)
                    o_lat = jnp.transpose(o_lat.reshape(bd, t_len, n_heads, kv_lora), (2, 1, 0, 3))
                    o_mix = _head_proj(o_lat.reshape(n_heads, n_s, kv_lora), w_uvh)
                    sc_new[gi].append(_batch_major(tail, bd))
                    ckvs[gi].append(_batch_major(ckv, bd))
                    kpes[gi].append(_batch_major(kpe[:, :rope], bd))
                xs[gi], hs[gi] = _linear_residual_norm(
                    [ya, o_mix], w_out, jnp.zeros((d,), F32), xs[gi], nxt_gain,
                    tm=g["tm"], tk=tk_out, out_dtype=BF16)
        else:
            o = l // 2
            w1 = w_pw1[o].astype(BF16)
            w2 = w_pw2[o].astype(BF16)
            kw = conv_c.shape[1]
            for gi, g in enumerate(groups):
                hist = None if gi == 0 else _time_major(state_conf[o])
                cpre, tail = _glu_conv(hs[gi], w1, b_pw1[o], conv_c[o], conv_c_b[o], hist,
                                       tm=g["tm"], tc=tc_c, shift=g["shift"], tiles_per_seq=g["tps"])
                act = _ln_silu(cpre, ln_c_g[o], ln_c_b[o], g["tm"])
                cf_new[gi].append(prompt_tail(tail, kw - 1) if gi == 0 else _batch_major(tail, bd))
                xs[gi], hs[gi] = _linear_residual_norm(
                    [act], w2, b_pw2[o], xs[gi], nxt_gain, tm=g["tm"], tk=min(K_TILE, d), out_dtype=BF16)
        final = l == depth - 1
        gain = norm_final if final else norm_mix[l + 1]
        w_up_b = w_up[l].astype(BF16)
        w_down_b = w_down[l].astype(BF16)
        for gi, g in enumerate(groups):
            hist = None if gi == 0 else _time_major(state_ffn[l])
            xs[gi], hs[gi], tail_g, tail_u = _conv_ffn(
                hs[gi], xs[gi], w_up_b, conv_f[l], w_down_b, gain, hist, tm=g["tm"], tf=tf,
                shift=g["shift"], tiles_per_seq=g["tps"], out_dtype=F32 if final else BF16)
            if gi == 0:
                ff_new[gi].append(jnp.concatenate([prompt_tail(tail_g, 2), prompt_tail(tail_u, 2)], axis=-1))
            else:
                ff_new[gi].append(_batch_major(jnp.concatenate([tail_g, tail_u], axis=-1), bd))

    y_prompt = hs[0].reshape(bp, seq, d)
    y_sample = _batch_major(hs[1], bd)
    st = lambda parts: jnp.stack(parts)
    return (y_prompt, y_sample, st(ckvs[0]), st(kpes[0]), st(ckvs[1]), st(kpes[1]),
            st(sc_new[0]), st(sc_new[1]), st(cf_new[0]), st(cf_new[1]), st(ff_new[0]), st(ff_new[1]))
```

```python
import functools

import jax
import jax.numpy as jnp
from jax import lax
from jax.experimental import pallas as pl
from jax.experimental.pallas import tpu as pltpu

F32 = jnp.float32
BF16 = jnp.bfloat16
EPS = 1e-6
ROPE_THETA = 10000.0

V7X_LANES = 128
V7X_SUBLANES = 8
V7X_VMEM_BYTES = 64 * 1024 * 1024
VMEM_LIMIT_BYTES = V7X_VMEM_BYTES - 6 * 1024 * 1024

TOKEN_TILE = 512
FF_TILE = 256
CH_TILE = 256
OUT_TILE = 512
ATTN_Q_TILE = 256
ATTN_K_TILE = 256
PROMPT_HIST = V7X_SUBLANES
SAMPLE_KEYS_PER_STEP = 4096
ACC_COLS = 512
NORM_ROWS = 128
CONV_ROWS = 64


def _cparams(sem):
    return pltpu.CompilerParams(dimension_semantics=sem, vmem_limit_bytes=VMEM_LIMIT_BYTES)


def _sigmoid(x):
    return 1.0 / (1.0 + jnp.exp(-x))


def _dot(a, b):
    return jnp.dot(a, b, preferred_element_type=F32)


def _dot_nt(a, b):
    return lax.dot_general(a, b, (((1,), (1,)), ((), ())), preferred_element_type=F32)


def _rms_rows(x, gain):
    return x * lax.rsqrt(jnp.mean(x * x, axis=-1, keepdims=True) + EPS) * gain


def _accumulate_dot(acc_ref, a, w_ref):
    d = acc_ref.shape[1]
    step = min(d, ACC_COLS)
    for n0 in range(0, d, step):
        cols = slice(n0, n0 + step)
        acc_ref[:, cols] += _dot(a, w_ref[:, cols])


def _store_rms(dst_ref, src_ref, gain_ref):
    tm = src_ref.shape[0]
    step = min(tm, NORM_ROWS)
    for r0 in range(0, tm, step):
        rows = pl.ds(r0, step)
        dst_ref[rows, :] = _rms_rows(src_ref[rows, :], gain_ref[...]).astype(dst_ref.dtype)


def _causal_taps(full_ref, w_ref, hist_rows, tm, shift):
    kw = w_ref.shape[0]
    acc = None
    for k in range(kw):
        off = hist_rows - (kw - 1 - k) * shift
        term = full_ref[pl.ds(off, tm), :] * w_ref[pl.ds(k, 1), :]
        acc = term if acc is None else acc + term
    return acc


def _causal_taps_banded(full_ref, w_ref, hist_rows, tm, shift, store):
    kw = w_ref.shape[0]
    band = min(tm, CONV_ROWS)
    for r0 in range(0, tm, band):
        acc = None
        for k in range(kw):
            off = hist_rows - (kw - 1 - k) * shift
            term = full_ref[pl.ds(off + r0, band), :] * w_ref[pl.ds(k, 1), :]
            acc = term if acc is None else acc + term
        store(r0, band, acc)


def _causal_taps_rowshift(full_ref, w_ref, shifted_ref, hist_rows, tm, store):
    kw = w_ref.shape[0]
    span = shifted_ref.shape[1]
    for r in range(1, V7X_SUBLANES):
        shifted_ref[r - 1] = full_ref[pl.ds(r, span), :]
    band = min(tm, CONV_ROWS)
    for r0 in range(0, tm, band):
        acc = None
        for k in range(kw):
            off = hist_rows - (kw - 1 - k)
            r = off % V7X_SUBLANES
            if r == 0:
                src = full_ref[pl.ds(off + r0, band), :]
            else:
                src = shifted_ref[r - 1, pl.ds(off - r + r0, band), :]
            term = src * w_ref[pl.ds(k, 1), :]
            acc = term if acc is None else acc + term
        store(r0, band, acc)


def _stage_history(full_ref, hist_ref, carry_ref, tail_ref, chunk, first_of_seq, hist_rows, tm):
    if hist_ref is not None:
        full_ref[pl.ds(0, hist_rows), :] = hist_ref[...]
        tail_rows = tail_ref.shape[-2]
        tail_ref[...] = full_ref[pl.ds(hist_rows + tm - tail_rows, tail_rows), :]
        return
    prev = carry_ref[chunk]
    full_ref[pl.ds(0, hist_rows), :] = jnp.where(first_of_seq, jnp.zeros_like(prev), prev)
    tail = full_ref[pl.ds(tm, hist_rows), :]
    carry_ref[chunk] = tail
    tail_ref[...] = tail


def _chunk_pipeline(c, n_c, produce, consume, prologue=None, after_first=None, epilogue=None):
    even = lax.rem(c, 2) == 0
    mid = jnp.logical_and(c > 0, c < n_c)

    @pl.when(c == 0)
    def _():
        if prologue is not None:
            prologue()
        produce(0)
        if after_first is not None:
            after_first()

    @pl.when(jnp.logical_and(mid, even))
    def _():
        produce(0)
        consume(1)

    @pl.when(jnp.logical_and(mid, jnp.logical_not(even)))
    def _():
        produce(1)
        consume(0)

    @pl.when(c == n_c)
    def _():
        consume((n_c - 1) % 2)
        if epilogue is not None:
            epilogue()


def _zero_carries_once(i, carry_refs):
    @pl.when(i == 0)
    def _():
        for ref in carry_refs:
            ref[...] = jnp.zeros(ref.shape, F32)


def _prev(c):
    return jnp.maximum(c - 1, 0)


def _rms_body(x_ref, g_ref, o_ref):
    _store_rms(o_ref, x_ref, g_ref)


def _rmsnorm(x, gain, tm):
    n, d = x.shape
    return pl.pallas_call(
        _rms_body,
        grid=(n // tm,),
        in_specs=[pl.BlockSpec((tm, d), lambda i: (i, 0)), pl.BlockSpec((1, d), lambda i: (0, 0))],
        out_specs=pl.BlockSpec((tm, d), lambda i: (i, 0)),
        out_shape=jax.ShapeDtypeStruct((n, d), BF16),
        compiler_params=_cparams(("arbitrary",)),
        name="rmsnorm",
    )(x, gain.reshape(1, d))


def _ln_silu_body(c_ref, g_ref, b_ref, o_ref):
    tm = c_ref.shape[0]
    step = min(tm, NORM_ROWS)
    for r0 in range(0, tm, step):
        rows = pl.ds(r0, step)
        c = c_ref[rows, :]
        mu = jnp.mean(c, axis=-1, keepdims=True)
        xc = c - mu
        y = xc * lax.rsqrt(jnp.mean(xc * xc, axis=-1, keepdims=True) + EPS)
        y = y * g_ref[...] + b_ref[...]
        o_ref[rows, :] = (y * _sigmoid(y)).astype(o_ref.dtype)


def _ln_silu(c, gain, bias, tm):
    n, d = c.shape
    return pl.pallas_call(
        _ln_silu_body,
        grid=(n // tm,),
        in_specs=[pl.BlockSpec((tm, d), lambda i: (i, 0)),
                  pl.BlockSpec((1, d), lambda i: (0, 0)),
                  pl.BlockSpec((1, d), lambda i: (0, 0))],
        out_specs=pl.BlockSpec((tm, d), lambda i: (i, 0)),
        out_shape=jax.ShapeDtypeStruct((n, d), BF16),
        compiler_params=_cparams(("arbitrary",)),
        name="ln_silu",
    )(c, gain.reshape(1, d), bias.reshape(1, d))


def _lrn_body(*refs, n_a, n_n, tn):
    a_refs = refs[:n_a]
    w_refs = refs[n_a:2 * n_a]
    b_ref, x_ref, gn_ref, xo_ref, hn_ref, xs_ref = refs[2 * n_a:]
    n = pl.program_id(1)
    y = x_ref[...] + b_ref[...]
    for a_ref, w_ref in zip(a_refs, w_refs):
        y = y + _dot(a_ref[...], w_ref[...])
    xo_ref[...] = y
    xs_ref[n] = y

    @pl.when(n == n_n - 1)
    def _():
        tm = xs_ref.shape[1]
        d = n_n * tn
        step = min(tm, NORM_ROWS)
        for r0 in range(0, tm, step):
            rows = pl.ds(r0, step)
            ssq = None
            for cc in range(n_n):
                v = xs_ref[cc, rows, :]
                part = jnp.sum(v * v, axis=-1, keepdims=True)
                ssq = part if ssq is None else ssq + part
            r = lax.rsqrt(ssq * (1.0 / d) + EPS)
            for cc in range(n_n):
                cols = slice(cc * tn, (cc + 1) * tn)
                hn_ref[rows, cols] = (xs_ref[cc, rows, :] * r * gn_ref[:, cols]).astype(hn_ref.dtype)


def _linear_residual_norm(a_list, w, layer, bias, x, gain, tm, tn):
    n, d = x.shape
    n_a = len(a_list)
    k_each = a_list[0].shape[1]
    n_n = d // tn
    a_specs = [pl.BlockSpec((tm, k_each), lambda i, nn: (i, 0)) for _ in range(n_a)]
    w_specs = [pl.BlockSpec((None, k_each, tn), lambda i, nn, p=p: (layer, p, nn)) for p in range(n_a)]
    return pl.pallas_call(
        functools.partial(_lrn_body, n_a=n_a, n_n=n_n, tn=tn),
        grid=(n // tm, n_n),
        in_specs=a_specs + w_specs + [
            pl.BlockSpec((1, tn), lambda i, nn: (0, nn)),
            pl.BlockSpec((tm, tn), lambda i, nn: (i, nn)),
            pl.BlockSpec((1, d), lambda i, nn: (0, 0)),
        ],
        out_specs=[pl.BlockSpec((tm, tn), lambda i, nn: (i, nn)),
                   pl.BlockSpec((tm, d), lambda i, nn: (i, 0))],
        out_shape=[jax.ShapeDtypeStruct((n, d), F32), jax.ShapeDtypeStruct((n, d), BF16)],
        scratch_shapes=[pltpu.VMEM((n_n, tm, tn), F32)],
        compiler_params=_cparams(("arbitrary", "arbitrary")),
        name="linear_residual_norm",
    )(*a_list, *([w] * n_a), bias.reshape(1, d), x, gain.reshape(1, d))


def _ffn_body(*refs, tm, n_j, hist_rows, shift, tiles_per_seq, sample):
    if sample:
        (h_ref, x_hbm, wg_ref, wu_ref, cg_ref, cu_ref, wd_ref, gn_ref, hg_ref, hu_ref,
         xo_ref, hn_ref, tg_ref, tu_ref, fg0, fg1, fu0, fu1, sem) = refs
        cag_ref = cau_ref = None
    else:
        (h_ref, x_hbm, wg_ref, wu_ref, cg_ref, cu_ref, wd_ref, gn_ref,
         xo_ref, hn_ref, tg_ref, tu_ref, fg0, fg1, fu0, fu1, cag_ref, cau_ref, sem) = refs
        hg_ref = hu_ref = None
    fg = (fg0, fg1)
    fu = (fu0, fu1)
    i = pl.program_id(0)
    j = pl.program_id(1)
    first = lax.rem(i, tiles_per_seq) == 0

    def x_copy():
        return pltpu.make_async_copy(x_hbm.at[pl.ds(i * tm, tm), :], xo_ref, sem)

    def prologue():
        x_copy().start()
        if not sample:
            _zero_carries_once(i, (cag_ref, cau_ref))

    def produce(b):
        h = h_ref[...]
        fg[b][pl.ds(hist_rows, tm), :] = _dot(h, wg_ref[...])
        fu[b][pl.ds(hist_rows, tm), :] = _dot(h, wu_ref[...])

    def consume(b):
        chunk = j - 1
        _stage_history(fg[b], hg_ref, cag_ref, tg_ref, chunk, first, hist_rows, tm)
        _stage_history(fu[b], hu_ref, cau_ref, tu_ref, chunk, first, hist_rows, tm)
        cg = _causal_taps(fg[b], cg_ref, hist_rows, tm, shift)
        cu = _causal_taps(fu[b], cu_ref, hist_rows, tm, shift)
        act = (cg * _sigmoid(cg) * cu).astype(BF16)
        _accumulate_dot(xo_ref, act, wd_ref)

    _chunk_pipeline(j, n_j, produce, consume, prologue=prologue,
                    after_first=lambda: x_copy().wait(),
                    epilogue=lambda: _store_rms(hn_ref, xo_ref, gn_ref))


def _conv_ffn(h, x, w_up, conv_w, w_down, layer, gain, hist, *, tm, tf, shift, tiles_per_seq, out_dtype):
    n, d = x.shape
    d_ff = w_down.shape[1]
    n_j = d_ff // tf
    n_i = n // tm
    sample = hist is not None
    hist_rows = 2 * shift if sample else PROMPT_HIST
    cur = lambda j: jnp.minimum(j, n_j - 1)
    in_specs = [
        pl.BlockSpec((tm, d), lambda i, j: (i, 0)),
        pl.BlockSpec(memory_space=pl.ANY),
        pl.BlockSpec((None, d, tf), lambda i, j: (layer, 0, cur(j))),
        pl.BlockSpec((None, d, tf), lambda i, j: (layer, 0, n_j + cur(j))),
        pl.BlockSpec((None, 3, tf), lambda i, j: (layer, 0, _prev(j))),
        pl.BlockSpec((None, 3, tf), lambda i, j: (layer, 0, n_j + _prev(j))),
        pl.BlockSpec((None, tf, d), lambda i, j: (layer, _prev(j), 0)),
        pl.BlockSpec((1, d), lambda i, j: (0, 0)),
    ]
    args = [h, x, w_up, w_up, conv_w, conv_w, w_down, gain.reshape(1, d)]
    scratch = [pltpu.VMEM((hist_rows + tm, tf), F32) for _ in range(4)]
    if sample:
        in_specs += [pl.BlockSpec((hist_rows, tf), lambda i, j: (0, _prev(j))),
                     pl.BlockSpec((hist_rows, tf), lambda i, j: (0, n_j + _prev(j)))]
        args += [hist, hist]
        tail_shape = jax.ShapeDtypeStruct((hist_rows, d_ff), F32)
        tail_spec = pl.BlockSpec((hist_rows, tf), lambda i, j: (0, _prev(j)))
    else:
        scratch += [pltpu.VMEM((n_j, hist_rows, tf), F32), pltpu.VMEM((n_j, hist_rows, tf), F32)]
        tail_shape = jax.ShapeDtypeStruct((n_i, hist_rows, d_ff), F32)
        tail_spec = pl.BlockSpec((None, hist_rows, tf), lambda i, j: (i, 0, _prev(j)))
    scratch.append(pltpu.SemaphoreType.DMA(()))
    return pl.pallas_call(
        functools.partial(_ffn_body, tm=tm, n_j=n_j, hist_rows=hist_rows, shift=shift,
                          tiles_per_seq=tiles_per_seq, sample=sample),
        grid=(n_i, n_j + 1),
        in_specs=in_specs,
        out_specs=[pl.BlockSpec((tm, d), lambda i, j: (i, 0)),
                   pl.BlockSpec((tm, d), lambda i, j: (i, 0)),
                   tail_spec, tail_spec],
        out_shape=[jax.ShapeDtypeStruct((n, d), F32), jax.ShapeDtypeStruct((n, d), out_dtype),
                   tail_shape, tail_shape],
        scratch_shapes=scratch,
        compiler_params=_cparams(("arbitrary", "arbitrary")),
        name="conv_ffn",
    )(*args)


def _sconv_body(*refs, tm, n_c, hist_rows, shift, tiles_per_seq, sample):
    if sample:
        h_ref, wb_ref, wc_ref, wx_ref, cw_ref, hist_ref, ya_ref, tail_ref, f0, f1, z0, z1 = refs
        carry_ref = None
    else:
        h_ref, wb_ref, wc_ref, wx_ref, cw_ref, ya_ref, tail_ref, f0, f1, z0, z1, carry_ref = refs
        hist_ref = None
    full = (f0, f1)
    zb = (z0, z1)
    i = pl.program_id(0)
    c = pl.program_id(1)
    first = lax.rem(i, tiles_per_seq) == 0

    def prologue():
        if not sample:
            _zero_carries_once(i, (carry_ref,))

    def produce(b):
        h = h_ref[...]
        full[b][pl.ds(hist_rows, tm), :] = _dot(h, wc_ref[...]) * _dot(h, wx_ref[...])
        zb[b][...] = _dot(h, wb_ref[...])

    def consume(b):
        _stage_history(full[b], hist_ref, carry_ref, tail_ref, c - 1, first, hist_rows, tm)
        conv = _causal_taps(full[b], cw_ref, hist_rows, tm, shift)
        ya_ref[...] = (zb[b][...] * conv).astype(ya_ref.dtype)

    _chunk_pipeline(c, n_c, produce, consume, prologue=prologue)


def _short_conv(h, w_in, conv_w, layer, hist, *, d_a, tm, tc, shift, tiles_per_seq):
    n, d = h.shape
    n_c = d_a // tc
    n_i = n // tm
    sample = hist is not None
    hist_rows = 2 * shift if sample else PROMPT_HIST
    cur = lambda c: jnp.minimum(c, n_c - 1)
    in_specs = [
        pl.BlockSpec((tm, d), lambda i, c: (i, 0)),
        pl.BlockSpec((None, d, tc), lambda i, c: (layer, 0, cur(c))),
        pl.BlockSpec((None, d, tc), lambda i, c: (layer, 0, n_c + cur(c))),
        pl.BlockSpec((None, d, tc), lambda i, c: (layer, 0, 2 * n_c + cur(c))),
        pl.BlockSpec((None, 3, tc), lambda i, c: (layer, 0, _prev(c))),
    ]
    args = [h, w_in, w_in, w_in, conv_w]
    scratch = [pltpu.VMEM((hist_rows + tm, tc), F32), pltpu.VMEM((hist_rows + tm, tc), F32),
               pltpu.VMEM((tm, tc), F32), pltpu.VMEM((tm, tc), F32)]
    if sample:
        in_specs.append(pl.BlockSpec((hist_rows, tc), lambda i, c: (0, _prev(c))))
        args.append(hist)
        tail_shape = jax.ShapeDtypeStruct((hist_rows, d_a), F32)
        tail_spec = pl.BlockSpec((hist_rows, tc), lambda i, c: (0, _prev(c)))
    else:
        scratch.append(pltpu.VMEM((n_c, hist_rows, tc), F32))
        tail_shape = jax.ShapeDtypeStruct((n_i, hist_rows, d_a), F32)
        tail_spec = pl.BlockSpec((None, hist_rows, tc), lambda i, c: (i, 0, _prev(c)))
    return pl.pallas_call(
        functools.partial(_sconv_body, tm=tm, n_c=n_c, hist_rows=hist_rows, shift=shift,
                          tiles_per_seq=tiles_per_seq, sample=sample),
        grid=(n_i, n_c + 1),
        in_specs=in_specs,
        out_specs=[pl.BlockSpec((tm, tc), lambda i, c: (i, _prev(c))), tail_spec],
        out_shape=[jax.ShapeDtypeStruct((n, d_a), BF16), tail_shape],
        scratch_shapes=scratch,
        compiler_params=_cparams(("arbitrary", "arbitrary")),
        name="short_conv",
    )(*args)


def _glu_conv_body(*refs, tm, n_c, hist_rows, shift, tiles_per_seq, sample):
    if sample:
        (h_ref, wlo_ref, whi_ref, blo_ref, bhi_ref, cw_ref, cb_ref, hist_ref,
         c_ref, tail_ref, f0, f1) = refs
        carry_ref = shifted_ref = None
    else:
        (h_ref, wlo_ref, whi_ref, blo_ref, bhi_ref, cw_ref, cb_ref,
         c_ref, tail_ref, f0, f1, carry_ref, shifted_ref) = refs
        hist_ref = None
    full = (f0, f1)
    i = pl.program_id(0)
    c = pl.program_id(1)
    first = lax.rem(i, tiles_per_seq) == 0

    def prologue():
        if not sample:
            _zero_carries_once(i, (carry_ref,))

    def produce(b):
        h = h_ref[...]
        a_lo = _dot(h, wlo_ref[...]) + blo_ref[...]
        a_hi = _dot(h, whi_ref[...]) + bhi_ref[...]
        full[b][pl.ds(hist_rows, tm), :] = a_lo * _sigmoid(a_hi)

    def consume(b):
        _stage_history(full[b], hist_ref, carry_ref, tail_ref, c - 1, first, hist_rows, tm)
        def store(r0, rows, acc):
            c_ref[pl.ds(r0, rows), :] = acc + cb_ref[...]

        if sample:
            _causal_taps_banded(full[b], cw_ref, hist_rows, tm, shift, store)
        else:
            _causal_taps_rowshift(full[b], cw_ref, shifted_ref, hist_rows, tm, store)

    _chunk_pipeline(c, n_c, produce, consume, prologue=prologue)


def _glu_conv(h, w1, b1, conv_w, conv_b, layer, hist, *, tm, tc, shift, tiles_per_seq):
    n, d = h.shape
    _, kw, d_c = conv_w.shape
    n_c = d_c // tc
    n_i = n // tm
    sample = hist is not None
    hist_rows = (kw - 1) * shift if sample else -(-(kw - 1) // V7X_SUBLANES) * V7X_SUBLANES
    cur = lambda c: jnp.minimum(c, n_c - 1)
    in_specs = [
        pl.BlockSpec((tm, d), lambda i, c: (i, 0)),
        pl.BlockSpec((None, d, tc), lambda i, c: (layer, 0, cur(c))),
        pl.BlockSpec((None, d, tc), lambda i, c: (layer, 0, n_c + cur(c))),
        pl.BlockSpec((None, 1, tc), lambda i, c: (layer, 0, cur(c))),
        pl.BlockSpec((None, 1, tc), lambda i, c: (layer, 0, n_c + cur(c))),
        pl.BlockSpec((None, kw, tc), lambda i, c: (layer, 0, _prev(c))),
        pl.BlockSpec((None, 1, tc), lambda i, c: (layer, 0, _prev(c))),
    ]
    b1r = b1.reshape(b1.shape[0], 1, 2 * d_c)
    args = [h, w1, w1, b1r, b1r, conv_w, conv_b.reshape(conv_b.shape[0], 1, d_c)]
    scratch = [pltpu.VMEM((hist_rows + tm, tc), F32), pltpu.VMEM((hist_rows + tm, tc), F32)]
    if sample:
        in_specs.append(pl.BlockSpec((hist_rows, tc), lambda i, c: (0, _prev(c))))
        args.append(hist)
        tail_shape = jax.ShapeDtypeStruct((hist_rows, d_c), F32)
        tail_spec = pl.BlockSpec((hist_rows, tc), lambda i, c: (0, _prev(c)))
    else:
        scratch.append(pltpu.VMEM((n_c, hist_rows, tc), F32))
        scratch.append(pltpu.VMEM((V7X_SUBLANES - 1, tm + hist_rows - V7X_SUBLANES, tc), F32))
        tail_shape = jax.ShapeDtypeStruct((n_i, hist_rows, d_c), F32)
        tail_spec = pl.BlockSpec((None, hist_rows, tc), lambda i, c: (i, 0, _prev(c)))
    return pl.pallas_call(
        functools.partial(_glu_conv_body, tm=tm, n_c=n_c, hist_rows=hist_rows, shift=shift,
                          tiles_per_seq=tiles_per_seq, sample=sample),
        grid=(n_i, n_c + 1),
        in_specs=in_specs,
        out_specs=[pl.BlockSpec((tm, tc), lambda i, c: (i, _prev(c))), tail_spec],
        out_shape=[jax.ShapeDtypeStruct((n, d_c), F32), tail_shape],
        scratch_shapes=scratch,
        compiler_params=_cparams(("arbitrary", "arbitrary")),
        name="glu_conv",
    )(*args)


def _qkv_body(h_ref, w_ref, qg_ref, kg_ref, cos_ref, sin_ref,
              qn_ref, ckv_ref, kpe_ref, kf_ref, *, q_lora, kv_lora):
    z = _dot(h_ref[...], w_ref[...])
    qn_ref[...] = _rms_rows(z[:, :q_lora], qg_ref[...]).astype(qn_ref.dtype)
    ckv = _rms_rows(z[:, q_lora:q_lora + kv_lora], kg_ref[...])
    o = q_lora + kv_lora
    kpe = (z[:, o:o + V7X_LANES] * cos_ref[...]
           + z[:, o + V7X_LANES:o + 2 * V7X_LANES] * sin_ref[...])
    ckv_ref[...] = ckv
    kpe_ref[...] = kpe
    kf_ref[:, :kv_lora] = ckv.astype(kf_ref.dtype)
    kf_ref[:, kv_lora:] = kpe.astype(kf_ref.dtype)


def _qkv_proj(h, w_qkv, q_gain, kv_gain, cos_t, sin_t, *, tm, rope_tiles):
    n, d = h.shape
    q_lora = q_gain.shape[0]
    kv_lora = kv_gain.shape[0]
    wn = w_qkv.shape[1]
    kfw = kv_lora + V7X_LANES
    return pl.pallas_call(
        functools.partial(_qkv_body, q_lora=q_lora, kv_lora=kv_lora),
        grid=(n // tm,),
        in_specs=[
            pl.BlockSpec((tm, d), lambda i: (i, 0)),
            pl.BlockSpec((d, wn), lambda i: (0, 0)),
            pl.BlockSpec((1, q_lora), lambda i: (0, 0)),
            pl.BlockSpec((1, kv_lora), lambda i: (0, 0)),
            pl.BlockSpec((tm, V7X_LANES), lambda i: (i % rope_tiles, 0)),
            pl.BlockSpec((tm, V7X_LANES), lambda i: (i % rope_tiles, 0)),
        ],
        out_specs=[
            pl.BlockSpec((tm, q_lora), lambda i: (i, 0)),
            pl.BlockSpec((tm, kv_lora), lambda i: (i, 0)),
            pl.BlockSpec((tm, V7X_LANES), lambda i: (i, 0)),
            pl.BlockSpec((tm, kfw), lambda i: (i, 0)),
        ],
        out_shape=[
            jax.ShapeDtypeStruct((n, q_lora), BF16),
            jax.ShapeDtypeStruct((n, kv_lora), F32),
            jax.ShapeDtypeStruct((n, V7X_LANES), F32),
            jax.ShapeDtypeStruct((n, kfw), BF16),
        ],
        compiler_params=_cparams(("arbitrary",)),
        name="qkv_proj",
    )(h, w_qkv, q_gain.reshape(1, q_lora), kv_gain.reshape(1, kv_lora), cos_t, sin_t)


def _q_up_body(qn_ref, wn_ref, wp_ref, wr_ref, uk_ref, cos_ref, sin_ref, o_ref, *, hp, nope, kv_lora):
    qn = qn_ref[...]
    q_nope = _dot(qn, wn_ref[...]).astype(BF16)
    pe = _dot(qn, wp_ref[...])
    rot = _dot(qn, wr_ref[...])
    cos = cos_ref[...]
    sin = sin_ref[...]
    for hh in range(hp):
        lat = _dot(q_nope[:, hh * nope:(hh + 1) * nope], uk_ref[hh])
        sl = slice(hh * V7X_LANES, (hh + 1) * V7X_LANES)
        o_ref[hh, :, :kv_lora] = lat.astype(o_ref.dtype)
        o_ref[hh, :, kv_lora:] = (pe[:, sl] * cos + rot[:, sl] * sin).astype(o_ref.dtype)


def _q_up(qn, w_nope, w_pe, w_rot, w_ukt, cos_t, sin_t, *, tm, rope_tiles, hp):
    n, q_lora = qn.shape
    n_heads, nope, kv_lora = w_ukt.shape
    qw = kv_lora + V7X_LANES
    return pl.pallas_call(
        functools.partial(_q_up_body, hp=hp, nope=nope, kv_lora=kv_lora),
        grid=(n // tm, n_heads // hp),
        in_specs=[
            pl.BlockSpec((tm, q_lora), lambda i, g: (i, 0)),
            pl.BlockSpec((q_lora, hp * nope), lambda i, g: (0, g)),
            pl.BlockSpec((q_lora, hp * V7X_LANES), lambda i, g: (0, g)),
            pl.BlockSpec((q_lora, hp * V7X_LANES), lambda i, g: (0, g)),
            pl.BlockSpec((hp, nope, kv_lora), lambda i, g: (g, 0, 0)),
            pl.BlockSpec((tm, V7X_LANES), lambda i, g: (i % rope_tiles, 0)),
            pl.BlockSpec((tm, V7X_LANES), lambda i, g: (i % rope_tiles, 0)),
        ],
        out_specs=pl.BlockSpec((hp, tm, qw), lambda i, g: (g, i, 0)),
        out_shape=jax.ShapeDtypeStruct((n_heads, n, qw), BF16),
        compiler_params=_cparams(("arbitrary", "arbitrary")),
        name="q_up",
    )(qn, w_nope, w_pe, w_rot, w_ukt, cos_t, sin_t)


def _softmax_update(s, v, m_ref, l_ref, acc_ref):
    m_prev = m_ref[...]
    m_new = jnp.maximum(m_prev, jnp.max(s, axis=-1, keepdims=True))
    alpha = jnp.exp(m_prev - m_new)
    p = jnp.exp(s - m_new)
    l_ref[...] = alpha * l_ref[...] + jnp.sum(p, axis=-1, keepdims=True)
    acc_ref[...] = alpha * acc_ref[...] + _dot(p.astype(BF16), v)
    m_ref[...] = m_new


def _attn_prompt_body(q_ref, k_ref, wv_ref, o_ref, m_ref, l_ref, acc_ref,
                      *, tq, tk, n_k, n_heads, kv_lora, v_dim, scale):
    qi = pl.program_id(1)
    ki = pl.program_id(2)
    first_masked = (qi * tq) // tk
    last = ((qi + 1) * tq - 1) // tk
    rows = n_heads * tq

    @pl.when(ki == 0)
    def _():
        m_ref[...] = jnp.full(m_ref.shape, -jnp.inf, F32)
        l_ref[...] = jnp.zeros(l_ref.shape, F32)
        acc_ref[...] = jnp.zeros(acc_ref.shape, F32)

    def scores():
        q = q_ref[...].reshape(rows, q_ref.shape[-1])
        return _dot_nt(q, k_ref[...]) * scale

    @pl.when(ki < first_masked)
    def _():
        _softmax_update(scores(), k_ref[:, :kv_lora], m_ref, l_ref, acc_ref)

    @pl.when(jnp.logical_and(ki >= first_masked, ki <= last))
    def _():
        s = scores()
        q_pos = qi * tq + lax.rem(lax.broadcasted_iota(jnp.int32, (rows, tk), 0), tq)
        k_pos = ki * tk + lax.broadcasted_iota(jnp.int32, (rows, tk), 1)
        s = jnp.where(k_pos <= q_pos, s, -jnp.inf)
        _softmax_update(s, k_ref[:, :kv_lora], m_ref, l_ref, acc_ref)

    @pl.when(ki == n_k - 1)
    def _():
        o_lat = (acc_ref[...] / l_ref[...]).astype(BF16)
        for hh in range(n_heads):
            o_ref[:, hh * v_dim:(hh + 1) * v_dim] = _dot(
                o_lat[hh * tq:(hh + 1) * tq], wv_ref[hh]).astype(o_ref.dtype)


def _attn_prompt(q, kf, w_uvh, *, batch, seq, tq, tk, scale):
    n_heads, n, qw = q.shape
    _, kv_lora, v_dim = w_uvh.shape
    n_q = seq // tq
    n_k = seq // tk
    rows = n_heads * tq

    def k_map(b, qi, ki):
        last = ((qi + 1) * tq - 1) // tk
        return (b * n_k + jnp.minimum(ki, last), 0)

    return pl.pallas_call(
        functools.partial(_attn_prompt_body, tq=tq, tk=tk, n_k=n_k, n_heads=n_heads,
                          kv_lora=kv_lora, v_dim=v_dim, scale=scale),
        grid=(batch, n_q, n_k),
        in_specs=[
            pl.BlockSpec((n_heads, tq, qw), lambda b, qi, ki: (0, b * n_q + qi, 0)),
            pl.BlockSpec((tk, qw), k_map),
            pl.BlockSpec((n_heads, kv_lora, v_dim), lambda b, qi, ki: (0, 0, 0)),
        ],
        out_specs=pl.BlockSpec((tq, n_heads * v_dim), lambda b, qi, ki: (b * n_q + qi, 0)),
        out_shape=jax.ShapeDtypeStruct((n, n_heads * v_dim), BF16),
        scratch_shapes=[pltpu.VMEM((rows, 1), F32), pltpu.VMEM((rows, 1), F32),
                        pltpu.VMEM((rows, kv_lora), F32)],
        compiler_params=_cparams(("arbitrary", "arbitrary", "arbitrary")),
        name="attn_prompt",
    )(q, kf, w_uvh)


def _attn_sample_body(pt_ref, q_ref, kn_ref, ckv_hbm, kpet_hbm, o_ref,
                      cbuf, pbuf, sem, m_ref, l_ref, acc_ref,
                      *, layer, pages, page, n_half, n_heads, n_new, kv_lora, rope, scale):
    b = pl.program_id(0)
    hf = pl.program_id(1)
    step = b * n_half + hf
    n_steps = pl.num_programs(0) * n_half
    slot = lax.rem(step, 2)

    def page_copies(bb, hh, sl):
        cps = []
        for p in range(pages):
            pg = pt_ref[bb, hh * pages + p]
            keys = pl.ds(p * page, page)
            cps.append(pltpu.make_async_copy(ckv_hbm.at[layer, pg], cbuf.at[sl, keys, :], sem.at[0, sl]))
            cps.append(pltpu.make_async_copy(kpet_hbm.at[layer, pg], pbuf.at[sl, :, keys], sem.at[1, sl]))
        return cps

    @pl.when(step == 0)
    def _():
        for cp in page_copies(b, hf, slot):
            cp.start()

    @pl.when(step + 1 < n_steps)
    def _():
        wrap = hf == n_half - 1
        nb = jnp.where(wrap, b + 1, b)
        nh = jnp.where(wrap, 0, hf + 1)
        for cp in page_copies(nb, nh, 1 - slot):
            cp.start()

    for cp in page_copies(b, hf, slot):
        cp.wait()

    @pl.when(hf == 0)
    def _():
        m_ref[...] = jnp.full(m_ref.shape, -jnp.inf, F32)
        l_ref[...] = jnp.zeros(l_ref.shape, F32)
        acc_ref[...] = jnp.zeros(acc_ref.shape, F32)

    q = q_ref[0]
    kc = cbuf[slot].astype(BF16)
    kpt = pbuf[slot].astype(BF16)
    s = (_dot_nt(q[:, :kv_lora], kc) + _dot(q[:, kv_lora:kv_lora + rope], kpt)) * scale
    _softmax_update(s, kc, m_ref, l_ref, acc_ref)

    @pl.when(hf == n_half - 1)
    def _():
        kn = kn_ref[0]
        sn = _dot_nt(q, kn) * scale
        t_row = lax.broadcasted_iota(jnp.int32, sn.shape, 0) // n_heads
        u_col = lax.broadcasted_iota(jnp.int32, sn.shape, 1)
        sn = jnp.where(jnp.logical_and(u_col <= t_row, u_col < n_new), sn, -jnp.inf)
        _softmax_update(sn, kn[:, :kv_lora], m_ref, l_ref, acc_ref)
        o_ref[0] = (acc_ref[...] / l_ref[...]).astype(o_ref.dtype)


def _attn_sample(page_table, q, k_new, cache_ckv, cache_kpe_t, *, layer, n_new, n_heads, scale):
    bd, rows, qw = q.shape
    _, _, page, kv_lora = cache_ckv.shape
    rope = cache_kpe_t.shape[2]
    n_pages = page_table.shape[1]
    past = n_pages * page
    keys = min(SAMPLE_KEYS_PER_STEP, past)
    n_half = past // keys
    pages = keys // page
    grid_spec = pltpu.PrefetchScalarGridSpec(
        num_scalar_prefetch=1,
        grid=(bd, n_half),
        in_specs=[
            pl.BlockSpec((1, rows, qw), lambda b, hf, pt: (b, 0, 0)),
            pl.BlockSpec((1, k_new.shape[1], qw), lambda b, hf, pt: (b, 0, 0)),
            pl.BlockSpec(memory_space=pl.ANY),
            pl.BlockSpec(memory_space=pl.ANY),
        ],
        out_specs=pl.BlockSpec((1, rows, kv_lora), lambda b, hf, pt: (b, 0, 0)),
        scratch_shapes=[
            pltpu.VMEM((2, keys, kv_lora), F32),
            pltpu.VMEM((2, rope, keys), F32),
            pltpu.SemaphoreType.DMA((2, 2)),
            pltpu.VMEM((rows, 1), F32),
            pltpu.VMEM((rows, 1), F32),
            pltpu.VMEM((rows, kv_lora), F32),
        ],
    )
    return pl.pallas_call(
        functools.partial(_attn_sample_body, layer=layer, pages=pages, page=page, n_half=n_half,
                          n_heads=n_heads, n_new=n_new, kv_lora=kv_lora, rope=rope, scale=scale),
        grid_spec=grid_spec,
        out_shape=jax.ShapeDtypeStruct((bd, rows, kv_lora), BF16),
        compiler_params=_cparams(("arbitrary", "arbitrary")),
        name="attn_sample",
    )(page_table, q, k_new, cache_ckv, cache_kpe_t)


def _head_proj_body(x_ref, w_ref, o_ref):
    o_ref[...] = _dot(x_ref[0], w_ref[0]).astype(o_ref.dtype)


def _head_proj(o_lat, w_uvh):
    n_heads, n, kv_lora = o_lat.shape
    v_dim = w_uvh.shape[-1]
    return pl.pallas_call(
        _head_proj_body,
        grid=(n_heads,),
        in_specs=[pl.BlockSpec((1, n, kv_lora), lambda h: (h, 0, 0)),
                  pl.BlockSpec((1, kv_lora, v_dim), lambda h: (h, 0, 0))],
        out_specs=pl.BlockSpec((n, v_dim), lambda h: (0, h)),
        out_shape=jax.ShapeDtypeStruct((n, n_heads * v_dim), BF16),
        compiler_params=_cparams(("arbitrary",)),
        name="head_proj",
    )(o_lat, w_uvh)


def _rope_tables(pos, rope):
    half = rope // 2
    inv = 1.0 / (ROPE_THETA ** (jnp.arange(half, dtype=F32) * (2.0 / rope)))
    ang = pos.astype(F32)[:, None] * inv[None, :]
    cos = jnp.cos(ang)
    sin = jnp.sin(ang)
    pad = jnp.zeros((pos.shape[0], V7X_LANES - rope), F32)
    return (jnp.concatenate([cos, cos, pad], axis=-1), jnp.concatenate([-sin, sin, pad], axis=-1))


def _swap_halves(w):
    half = w.shape[-1] // 2
    return jnp.concatenate([w[..., half:], w[..., :half]], axis=-1)


def _pad_lanes(w):
    pad = [(0, 0)] * (w.ndim - 1) + [(0, V7X_LANES - w.shape[-1])]
    return jnp.pad(w, pad)


def _time_major(a):
    b, t, c = a.shape
    return jnp.swapaxes(a, 0, 1).reshape(t * b, c)


def _batch_major(a, b):
    tb, c = a.shape
    return jnp.swapaxes(a.reshape(tb // b, b, c), 0, 1)


def kernel(x_prompt, x_sample, cache_ckv, cache_kpe, page_table, state_sconv, state_conf, state_ffn,
           norm_mix, norm_ffn, norm_final, w_in_e, conv_a, q_norm, w_uq, kv_norm, w_ukv, w_out_e,
           w_pw1, b_pw1, conv_c, conv_c_b, ln_c_g, ln_c_b, w_pw2, b_pw2, w_up, conv_f, w_down):
    bp, seq, d = x_prompt.shape
    bd, t_len, _ = x_sample.shape
    depth = norm_mix.shape[0]
    d_a = conv_a.shape[-1]
    q_lora = q_norm.shape[-1]
    kv_lora = kv_norm.shape[-1]
    rope = cache_kpe.shape[-1]
    n_heads = w_ukv.shape[2]
    nope = w_uq.shape[-1] // n_heads - rope
    v_dim = w_ukv.shape[-1] - nope
    page = cache_ckv.shape[2]
    past_len = page_table.shape[1] * page
    scale = float(nope + rope) ** -0.5
    n_p = bp * seq
    n_s = bd * t_len

    tm_p = min(TOKEN_TILE, seq)
    tps = seq // tm_p
    tq = min(ATTN_Q_TILE, seq)
    tk = min(ATTN_K_TILE, seq)
    tf = min(FF_TILE, w_down.shape[1])
    tc_a = min(CH_TILE, d_a)
    tc_c = min(CH_TILE, d)
    tn = min(OUT_TILE, d)

    cos_p, sin_p = _rope_tables(jnp.arange(seq), rope)
    cos_s, sin_s = _rope_tables(past_len + jnp.repeat(jnp.arange(t_len), bd), rope)

    groups = (
        dict(n=n_p, tm=tm_p, shift=1, tps=tps, cos=cos_p, sin=sin_p, rope_tiles=tps),
        dict(n=n_s, tm=n_s, shift=bd, tps=1, cos=cos_s, sin=sin_s, rope_tiles=1),
    )
    xs = [x_prompt.reshape(n_p, d), _time_major(x_sample)]
    hs = [_rmsnorm(x, norm_mix[0], g["tm"]) for x, g in zip(xs, groups)]

    w_in_b = w_in_e.astype(BF16)
    w_out_b = w_out_e.astype(BF16)
    w_pw1_b = w_pw1.astype(BF16)
    w_pw2_b = w_pw2.astype(BF16)
    w_up_b = w_up.astype(BF16)
    w_down_b = w_down.astype(BF16)
    cache_kpe_t = jnp.swapaxes(cache_kpe, 2, 3)
    zero_bias = jnp.zeros((d,), F32)

    sc_new, cf_new, ff_new, ckvs, kpes = [[], []], [[], []], [[], []], [[], []], [[], []]

    def prompt_tail(tail, rows):
        return tail[tps - 1::tps, tail.shape[1] - rows:, :]

    for l in range(depth):
        nxt_gain = norm_ffn[l]
        if l % 2 == 0:
            e = l // 2
            w_in = w_in_e[e]
            o = 3 * d_a
            w_kr = w_in[:, o + q_lora + kv_lora:]
            w_qkv = jnp.concatenate(
                [w_in[:, o:o + q_lora + kv_lora], _pad_lanes(w_kr), _pad_lanes(_swap_halves(w_kr))],
                axis=-1).astype(BF16)
            wq = w_uq[e].reshape(q_lora, n_heads, nope + rope)
            w_nope = wq[..., :nope].reshape(q_lora, n_heads * nope).astype(BF16)
            w_pe = _pad_lanes(wq[..., nope:]).reshape(q_lora, n_heads * V7X_LANES).astype(BF16)
            w_rot = _pad_lanes(_swap_halves(wq[..., nope:])).reshape(q_lora, n_heads * V7X_LANES).astype(BF16)
            w_ukt = jnp.transpose(w_ukv[e][..., :nope], (1, 2, 0)).astype(BF16)
            w_uvh = jnp.transpose(w_ukv[e][..., nope:], (1, 0, 2)).astype(BF16)
            for gi, g in enumerate(groups):
                hist = None if gi == 0 else _time_major(state_sconv[e])
                ya, tail = _short_conv(hs[gi], w_in_b, conv_a, e, hist, d_a=d_a, tm=g["tm"], tc=tc_a,
                                       shift=g["shift"], tiles_per_seq=g["tps"])
                qn, ckv, kpe, kf = _qkv_proj(hs[gi], w_qkv, q_norm[e], kv_norm[e], g["cos"], g["sin"],
                                             tm=g["tm"], rope_tiles=g["rope_tiles"])
                qf = _q_up(qn, w_nope, w_pe, w_rot, w_ukt, g["cos"], g["sin"],
                           tm=g["tm"], rope_tiles=g["rope_tiles"], hp=2)
                if gi == 0:
                    o_mix = _attn_prompt(qf, kf, w_uvh, batch=bp, seq=seq, tq=tq, tk=tk, scale=scale)
                    sc_new[gi].append(prompt_tail(tail, 2))
                    ckvs[gi].append(ckv.reshape(bp, seq, kv_lora))
                    kpes[gi].append(kpe[:, :rope].reshape(bp, seq, rope))
                else:
                    qw = qf.shape[-1]
                    q_seq = jnp.transpose(qf.reshape(n_heads, t_len, bd, qw), (2, 1, 0, 3))
                    q_seq = q_seq.reshape(bd, t_len * n_heads, qw)
                    k_new = jnp.pad(_batch_major(kf, bd), ((0, 0), (0, 2 * V7X_SUBLANES - t_len), (0, 0)))
                    o_lat = _attn_sample(page_table, q_seq, k_new, cache_ckv, cache_kpe_t, layer=e,
                                         n_new=t_len, n_heads=n_heads, scale=scale)
                    o_lat = jnp.transpose(o_lat.reshape(bd, t_len, n_heads, kv_lora), (2, 1, 0, 3))
                    o_mix = _head_proj(o_lat.reshape(n_heads, n_s, kv_lora), w_uvh)
                    sc_new[gi].append(_batch_major(tail, bd))
                    ckvs[gi].append(_batch_major(ckv, bd))
                    kpes[gi].append(_batch_major(kpe[:, :rope], bd))
                xs[gi], hs[gi] = _linear_residual_norm(
                    [ya, o_mix], w_out_b, e, zero_bias, xs[gi], nxt_gain, tm=g["tm"], tn=tn)
        else:
            o = l // 2
            kw = conv_c.shape[1]
            for gi, g in enumerate(groups):
                hist = None if gi == 0 else _time_major(state_conf[o])
                cpre, tail = _glu_conv(hs[gi], w_pw1_b, b_pw1, conv_c, conv_c_b, o, hist,
                                       tm=g["tm"], tc=tc_c, shift=g["shift"], tiles_per_seq=g["tps"])
                act = _ln_silu(cpre, ln_c_g[o], ln_c_b[o], g["tm"])
                cf_new[gi].append(prompt_tail(tail, kw - 1) if gi == 0 else _batch_major(tail, bd))
                xs[gi], hs[gi] = _linear_residual_norm(
                    [act], w_pw2_b, o, b_pw2[o], xs[gi], nxt_gain, tm=g["tm"], tn=tn)
        final = l == depth - 1
        gain = norm_final if final else norm_mix[l + 1]
        for gi, g in enumerate(groups):
            hist = None if gi == 0 else _time_major(state_ffn[l])
            xs[gi], hs[gi], tail_g, tail_u = _conv_ffn(
                hs[gi], xs[gi], w_up_b, conv_f, w_down_b, l, gain, hist, tm=g["tm"], tf=tf,
                shift=g["shift"], tiles_per_seq=g["tps"], out_dtype=F32 if final else BF16)
            if gi == 0:
                ff_new[gi].append(jnp.concatenate([prompt_tail(tail_g, 2), prompt_tail(tail_u, 2)], axis=-1))
            else:
                ff_new[gi].append(_batch_major(jnp.concatenate([tail_g, tail_u], axis=-1), bd))

    y_prompt = hs[0].reshape(bp, seq, d)
    y_sample = _batch_major(hs[1], bd)
    st = lambda parts: jnp.stack(parts)
    return (y_prompt, y_sample, st(ckvs[0]), st(kpes[0]), st(ckvs[1]), st(kpes[1]),
            st(sc_new[0]), st(sc_new[1]), st(cf_new[0]), st(cf_new[1]), st(ff_new[0]), st(ff_new[1]))
```

```python
import functools

import jax
import jax.numpy as jnp
from jax import lax
from jax.experimental import pallas as pl
from jax.experimental.pallas import tpu as pltpu

F32 = jnp.float32
BF16 = jnp.bfloat16
EPS = 1e-6
ROPE_THETA = 10000.0

V7X_LANES = 128
V7X_SUBLANES = 8
V7X_VMEM_BYTES = 64 * 1024 * 1024
VMEM_LIMIT_BYTES = V7X_VMEM_BYTES - 6 * 1024 * 1024

TOKEN_TILE = 512
FF_TILE = 256
CH_TILE = 256
OUT_TILE = 1024
Q_UP_TILE = 1024
ATTN_Q_TILE = 256
ATTN_K_TILE = 512
PROMPT_HIST = V7X_SUBLANES
SAMPLE_KEYS_PER_STEP = 4096
ACC_COLS = 512
NORM_ROWS = 128
CONV_ROWS = 64


def _cparams(sem):
    return pltpu.CompilerParams(dimension_semantics=sem, vmem_limit_bytes=VMEM_LIMIT_BYTES)


def _sigmoid(x):
    return 1.0 / (1.0 + jnp.exp(-x))


def _dot(a, b):
    return jnp.dot(a, b, preferred_element_type=F32)


def _dot_nt(a, b):
    return lax.dot_general(a, b, (((1,), (1,)), ((), ())), preferred_element_type=F32)


def _rms_rows(x, gain):
    return x * lax.rsqrt(jnp.mean(x * x, axis=-1, keepdims=True) + EPS) * gain


def _accumulate_dot(acc_ref, a, w_ref):
    d = acc_ref.shape[1]
    step = min(d, ACC_COLS)
    for n0 in range(0, d, step):
        cols = slice(n0, n0 + step)
        acc_ref[:, cols] += _dot(a, w_ref[:, cols])


def _store_rms(dst_ref, src_ref, gain_ref):
    tm = src_ref.shape[0]
    step = min(tm, NORM_ROWS)
    for r0 in range(0, tm, step):
        rows = pl.ds(r0, step)
        dst_ref[rows, :] = _rms_rows(src_ref[rows, :], gain_ref[...]).astype(dst_ref.dtype)


def _causal_taps(full_ref, w_ref, hist_rows, tm, shift):
    kw = w_ref.shape[0]
    acc = None
    for k in range(kw):
        off = hist_rows - (kw - 1 - k) * shift
        term = full_ref[pl.ds(off, tm), :] * w_ref[pl.ds(k, 1), :]
        acc = term if acc is None else acc + term
    return acc


def _causal_taps_banded(full_ref, w_ref, hist_rows, tm, shift, store):
    kw = w_ref.shape[0]
    band = min(tm, CONV_ROWS)
    for r0 in range(0, tm, band):
        acc = None
        for k in range(kw):
            off = hist_rows - (kw - 1 - k) * shift
            term = full_ref[pl.ds(off + r0, band), :] * w_ref[pl.ds(k, 1), :]
            acc = term if acc is None else acc + term
        store(r0, band, acc)


def _causal_taps_rowshift(full_ref, w_ref, shifted_ref, hist_rows, tm, store):
    kw = w_ref.shape[0]
    span = shifted_ref.shape[1]
    for r in range(1, V7X_SUBLANES):
        shifted_ref[r - 1] = full_ref[pl.ds(r, span), :]
    band = min(tm, CONV_ROWS)
    for r0 in range(0, tm, band):
        acc = None
        for k in range(kw):
            off = hist_rows - (kw - 1 - k)
            r = off % V7X_SUBLANES
            if r == 0:
                src = full_ref[pl.ds(off + r0, band), :]
            else:
                src = shifted_ref[r - 1, pl.ds(off - r + r0, band), :]
            term = src * w_ref[pl.ds(k, 1), :]
            acc = term if acc is None else acc + term
        store(r0, band, acc)


def _stage_history(full_ref, hist_ref, carry_ref, tail_ref, chunk, first_of_seq, hist_rows, tm):
    if hist_ref is not None:
        full_ref[pl.ds(0, hist_rows), :] = hist_ref[...]
        tail_rows = tail_ref.shape[-2]
        tail_ref[...] = full_ref[pl.ds(hist_rows + tm - tail_rows, tail_rows), :]
        return
    prev = carry_ref[chunk]
    full_ref[pl.ds(0, hist_rows), :] = jnp.where(first_of_seq, jnp.zeros_like(prev), prev)
    tail = full_ref[pl.ds(tm, hist_rows), :]
    carry_ref[chunk] = tail
    tail_ref[...] = tail


def _chunk_pipeline(c, n_c, produce, stage, finish, prologue=None, after_first=None, epilogue=None):
    even = lax.rem(c, 2) == 0
    mid = jnp.logical_and(c > 0, c < n_c)

    @pl.when(c == 0)
    def _():
        if prologue is not None:
            prologue()
        produce(0)
        if after_first is not None:
            after_first()

    @pl.when(jnp.logical_and(mid, even))
    def _():
        stage(1)
        produce(0)
        finish(1)

    @pl.when(jnp.logical_and(mid, jnp.logical_not(even)))
    def _():
        stage(0)
        produce(1)
        finish(0)

    @pl.when(c == n_c)
    def _():
        last = (n_c - 1) % 2
        stage(last)
        finish(last)
        if epilogue is not None:
            epilogue()


def _zero_carries_once(i, carry_refs):
    @pl.when(i == 0)
    def _():
        for ref in carry_refs:
            ref[...] = jnp.zeros(ref.shape, F32)


def _prev(c):
    return jnp.maximum(c - 1, 0)


def _rms_body(x_ref, g_ref, o_ref):
    _store_rms(o_ref, x_ref, g_ref)


def _rmsnorm(x, gain, tm):
    n, d = x.shape
    return pl.pallas_call(
        _rms_body,
        grid=(n // tm,),
        in_specs=[pl.BlockSpec((tm, d), lambda i: (i, 0)), pl.BlockSpec((1, d), lambda i: (0, 0))],
        out_specs=pl.BlockSpec((tm, d), lambda i: (i, 0)),
        out_shape=jax.ShapeDtypeStruct((n, d), BF16),
        compiler_params=_cparams(("arbitrary",)),
        name="rmsnorm",
    )(x, gain.reshape(1, d))


def _ln_silu_body(c_ref, g_ref, b_ref, o_ref):
    tm = c_ref.shape[0]
    step = min(tm, NORM_ROWS)
    for r0 in range(0, tm, step):
        rows = pl.ds(r0, step)
        c = c_ref[rows, :]
        mu = jnp.mean(c, axis=-1, keepdims=True)
        xc = c - mu
        y = xc * lax.rsqrt(jnp.mean(xc * xc, axis=-1, keepdims=True) + EPS)
        y = y * g_ref[...] + b_ref[...]
        o_ref[rows, :] = (y * _sigmoid(y)).astype(o_ref.dtype)


def _ln_silu(c, gain, bias, tm):
    n, d = c.shape
    return pl.pallas_call(
        _ln_silu_body,
        grid=(n // tm,),
        in_specs=[pl.BlockSpec((tm, d), lambda i: (i, 0)),
                  pl.BlockSpec((1, d), lambda i: (0, 0)),
                  pl.BlockSpec((1, d), lambda i: (0, 0))],
        out_specs=pl.BlockSpec((tm, d), lambda i: (i, 0)),
        out_shape=jax.ShapeDtypeStruct((n, d), BF16),
        compiler_params=_cparams(("arbitrary",)),
        name="ln_silu",
    )(c, gain.reshape(1, d), bias.reshape(1, d))


def _lrn_body(*refs, n_a, n_n, tn):
    a_refs = refs[:n_a]
    w_refs = refs[n_a:2 * n_a]
    b_ref, x_ref, gn_ref, xo_ref, hn_ref, xs_ref = refs[2 * n_a:]
    n = pl.program_id(1)
    y = x_ref[...] + b_ref[...]
    for a_ref, w_ref in zip(a_refs, w_refs):
        y = y + _dot(a_ref[...], w_ref[...])
    xo_ref[...] = y
    xs_ref[n] = y

    @pl.when(n == n_n - 1)
    def _():
        tm = xs_ref.shape[1]
        d = n_n * tn
        step = min(tm, NORM_ROWS)
        for r0 in range(0, tm, step):
            rows = pl.ds(r0, step)
            ssq = None
            for cc in range(n_n):
                v = xs_ref[cc, rows, :]
                part = jnp.sum(v * v, axis=-1, keepdims=True)
                ssq = part if ssq is None else ssq + part
            r = lax.rsqrt(ssq * (1.0 / d) + EPS)
            for cc in range(n_n):
                cols = slice(cc * tn, (cc + 1) * tn)
                hn_ref[rows, cols] = (xs_ref[cc, rows, :] * r * gn_ref[:, cols]).astype(hn_ref.dtype)


def _linear_residual_norm(a_list, w, layer, bias, x, gain, tm, tn):
    n, d = x.shape
    n_a = len(a_list)
    k_each = a_list[0].shape[1]
    n_n = d // tn
    a_specs = [pl.BlockSpec((tm, k_each), lambda i, nn: (i, 0)) for _ in range(n_a)]
    w_specs = [pl.BlockSpec((None, k_each, tn), lambda i, nn, p=p: (layer, p, nn)) for p in range(n_a)]
    return pl.pallas_call(
        functools.partial(_lrn_body, n_a=n_a, n_n=n_n, tn=tn),
        grid=(n // tm, n_n),
        in_specs=a_specs + w_specs + [
            pl.BlockSpec((1, tn), lambda i, nn: (0, nn)),
            pl.BlockSpec((tm, tn), lambda i, nn: (i, nn)),
            pl.BlockSpec((1, d), lambda i, nn: (0, 0)),
        ],
        out_specs=[pl.BlockSpec((tm, tn), lambda i, nn: (i, nn)),
                   pl.BlockSpec((tm, d), lambda i, nn: (i, 0))],
        out_shape=[jax.ShapeDtypeStruct((n, d), F32), jax.ShapeDtypeStruct((n, d), BF16)],
        scratch_shapes=[pltpu.VMEM((n_n, tm, tn), F32)],
        compiler_params=_cparams(("arbitrary", "arbitrary")),
        name="linear_residual_norm",
    )(*a_list, *([w] * n_a), bias.reshape(1, d), x, gain.reshape(1, d))


def _ffn_body(*refs, tm, n_j, hist_rows, shift, tiles_per_seq, sample):
    if sample:
        (h_ref, x_hbm, wg_ref, wu_ref, cg_ref, cu_ref, wd_ref, gn_ref, hg_ref, hu_ref,
         xo_ref, hn_ref, tg_ref, tu_ref, fg0, fg1, fu0, fu1, sem) = refs
        cag_ref = cau_ref = None
    else:
        (h_ref, x_hbm, wg_ref, wu_ref, cg_ref, cu_ref, wd_ref, gn_ref,
         xo_ref, hn_ref, tg_ref, tu_ref, fg0, fg1, fu0, fu1, cag_ref, cau_ref, sem) = refs
        hg_ref = hu_ref = None
    fg = (fg0, fg1)
    fu = (fu0, fu1)
    i = pl.program_id(0)
    j = pl.program_id(1)
    first = lax.rem(i, tiles_per_seq) == 0

    def x_copy():
        return pltpu.make_async_copy(x_hbm.at[pl.ds(i * tm, tm), :], xo_ref, sem)

    def prologue():
        x_copy().start()
        if not sample:
            _zero_carries_once(i, (cag_ref, cau_ref))

    def produce(b):
        h = h_ref[...]
        fg[b][pl.ds(hist_rows, tm), :] = _dot(h, wg_ref[...])
        fu[b][pl.ds(hist_rows, tm), :] = _dot(h, wu_ref[...])

    def stage(b):
        chunk = j - 1
        _stage_history(fg[b], hg_ref, cag_ref, tg_ref, chunk, first, hist_rows, tm)
        _stage_history(fu[b], hu_ref, cau_ref, tu_ref, chunk, first, hist_rows, tm)

    def finish(b):
        cg = _causal_taps(fg[b], cg_ref, hist_rows, tm, shift)
        cu = _causal_taps(fu[b], cu_ref, hist_rows, tm, shift)
        act = (cg * _sigmoid(cg) * cu).astype(BF16)
        _accumulate_dot(xo_ref, act, wd_ref)

    _chunk_pipeline(j, n_j, produce, stage, finish, prologue=prologue,
                    after_first=lambda: x_copy().wait(),
                    epilogue=lambda: _store_rms(hn_ref, xo_ref, gn_ref))


def _conv_ffn(h, x, w_up, conv_w, w_down, layer, gain, hist, *, tm, tf, shift, tiles_per_seq, out_dtype):
    n, d = x.shape
    d_ff = w_down.shape[1]
    n_j = d_ff // tf
    n_i = n // tm
    sample = hist is not None
    hist_rows = 2 * shift if sample else PROMPT_HIST
    cur = lambda j: jnp.minimum(j, n_j - 1)
    in_specs = [
        pl.BlockSpec((tm, d), lambda i, j: (i, 0)),
        pl.BlockSpec(memory_space=pl.ANY),
        pl.BlockSpec((None, d, tf), lambda i, j: (layer, 0, cur(j))),
        pl.BlockSpec((None, d, tf), lambda i, j: (layer, 0, n_j + cur(j))),
        pl.BlockSpec((None, 3, tf), lambda i, j: (layer, 0, _prev(j))),
        pl.BlockSpec((None, 3, tf), lambda i, j: (layer, 0, n_j + _prev(j))),
        pl.BlockSpec((None, tf, d), lambda i, j: (layer, _prev(j), 0)),
        pl.BlockSpec((1, d), lambda i, j: (0, 0)),
    ]
    args = [h, x, w_up, w_up, conv_w, conv_w, w_down, gain.reshape(1, d)]
    scratch = [pltpu.VMEM((hist_rows + tm, tf), F32) for _ in range(4)]
    if sample:
        in_specs += [pl.BlockSpec((hist_rows, tf), lambda i, j: (0, _prev(j))),
                     pl.BlockSpec((hist_rows, tf), lambda i, j: (0, n_j + _prev(j)))]
        args += [hist, hist]
        tail_shape = jax.ShapeDtypeStruct((hist_rows, d_ff), F32)
        tail_spec = pl.BlockSpec((hist_rows, tf), lambda i, j: (0, _prev(j)))
    else:
        scratch += [pltpu.VMEM((n_j, hist_rows, tf), F32), pltpu.VMEM((n_j, hist_rows, tf), F32)]
        tail_shape = jax.ShapeDtypeStruct((n_i, hist_rows, d_ff), F32)
        tail_spec = pl.BlockSpec((None, hist_rows, tf), lambda i, j: (i, 0, _prev(j)))
    scratch.append(pltpu.SemaphoreType.DMA(()))
    return pl.pallas_call(
        functools.partial(_ffn_body, tm=tm, n_j=n_j, hist_rows=hist_rows, shift=shift,
                          tiles_per_seq=tiles_per_seq, sample=sample),
        grid=(n_i, n_j + 1),
        in_specs=in_specs,
        out_specs=[pl.BlockSpec((tm, d), lambda i, j: (i, 0)),
                   pl.BlockSpec((tm, d), lambda i, j: (i, 0)),
                   tail_spec, tail_spec],
        out_shape=[jax.ShapeDtypeStruct((n, d), F32), jax.ShapeDtypeStruct((n, d), out_dtype),
                   tail_shape, tail_shape],
        scratch_shapes=scratch,
        compiler_params=_cparams(("arbitrary", "arbitrary")),
        name="conv_ffn",
    )(*args)


def _sconv_body(*refs, tm, n_c, hist_rows, shift, tiles_per_seq, sample):
    if sample:
        h_ref, wb_ref, wc_ref, wx_ref, cw_ref, hist_ref, ya_ref, tail_ref, f0, f1, z0, z1 = refs
        carry_ref = None
    else:
        h_ref, wb_ref, wc_ref, wx_ref, cw_ref, ya_ref, tail_ref, f0, f1, z0, z1, carry_ref = refs
        hist_ref = None
    full = (f0, f1)
    zb = (z0, z1)
    i = pl.program_id(0)
    c = pl.program_id(1)
    first = lax.rem(i, tiles_per_seq) == 0

    def prologue():
        if not sample:
            _zero_carries_once(i, (carry_ref,))

    def produce(b):
        h = h_ref[...]
        full[b][pl.ds(hist_rows, tm), :] = _dot(h, wc_ref[...]) * _dot(h, wx_ref[...])
        zb[b][...] = _dot(h, wb_ref[...])

    def stage(b):
        _stage_history(full[b], hist_ref, carry_ref, tail_ref, c - 1, first, hist_rows, tm)

    def finish(b):
        conv = _causal_taps(full[b], cw_ref, hist_rows, tm, shift)
        ya_ref[...] = (zb[b][...] * conv).astype(ya_ref.dtype)

    _chunk_pipeline(c, n_c, produce, stage, finish, prologue=prologue)


def _short_conv(h, w_in, conv_w, layer, hist, *, d_a, tm, tc, shift, tiles_per_seq):
    n, d = h.shape
    n_c = d_a // tc
    n_i = n // tm
    sample = hist is not None
    hist_rows = 2 * shift if sample else PROMPT_HIST
    cur = lambda c: jnp.minimum(c, n_c - 1)
    in_specs = [
        pl.BlockSpec((tm, d), lambda i, c: (i, 0)),
        pl.BlockSpec((None, d, tc), lambda i, c: (layer, 0, cur(c))),
        pl.BlockSpec((None, d, tc), lambda i, c: (layer, 0, n_c + cur(c))),
        pl.BlockSpec((None, d, tc), lambda i, c: (layer, 0, 2 * n_c + cur(c))),
        pl.BlockSpec((None, 3, tc), lambda i, c: (layer, 0, _prev(c))),
    ]
    args = [h, w_in, w_in, w_in, conv_w]
    scratch = [pltpu.VMEM((hist_rows + tm, tc), F32), pltpu.VMEM((hist_rows + tm, tc), F32),
               pltpu.VMEM((tm, tc), F32), pltpu.VMEM((tm, tc), F32)]
    if sample:
        in_specs.append(pl.BlockSpec((hist_rows, tc), lambda i, c: (0, _prev(c))))
        args.append(hist)
        tail_shape = jax.ShapeDtypeStruct((hist_rows, d_a), F32)
        tail_spec = pl.BlockSpec((hist_rows, tc), lambda i, c: (0, _prev(c)))
    else:
        scratch.append(pltpu.VMEM((n_c, hist_rows, tc), F32))
        tail_shape = jax.ShapeDtypeStruct((n_i, hist_rows, d_a), F32)
        tail_spec = pl.BlockSpec((None, hist_rows, tc), lambda i, c: (i, 0, _prev(c)))
    return pl.pallas_call(
        functools.partial(_sconv_body, tm=tm, n_c=n_c, hist_rows=hist_rows, shift=shift,
                          tiles_per_seq=tiles_per_seq, sample=sample),
        grid=(n_i, n_c + 1),
        in_specs=in_specs,
        out_specs=[pl.BlockSpec((tm, tc), lambda i, c: (i, _prev(c))), tail_spec],
        out_shape=[jax.ShapeDtypeStruct((n, d_a), BF16), tail_shape],
        scratch_shapes=scratch,
        compiler_params=_cparams(("arbitrary", "arbitrary")),
        name="short_conv",
    )(*args)


def _glu_conv_body(*refs, tm, n_c, hist_rows, shift, tiles_per_seq, sample):
    if sample:
        (h_ref, wlo_ref, whi_ref, blo_ref, bhi_ref, cw_ref, cb_ref, hist_ref,
         c_ref, tail_ref, f0, f1) = refs
        carry_ref = shifted_ref = None
    else:
        (h_ref, wlo_ref, whi_ref, blo_ref, bhi_ref, cw_ref, cb_ref,
         c_ref, tail_ref, f0, f1, carry_ref, shifted_ref) = refs
        hist_ref = None
    full = (f0, f1)
    i = pl.program_id(0)
    c = pl.program_id(1)
    first = lax.rem(i, tiles_per_seq) == 0

    def prologue():
        if not sample:
            _zero_carries_once(i, (carry_ref,))

    def produce(b):
        h = h_ref[...]
        a_lo = _dot(h, wlo_ref[...]) + blo_ref[...]
        a_hi = _dot(h, whi_ref[...]) + bhi_ref[...]
        full[b][pl.ds(hist_rows, tm), :] = a_lo * _sigmoid(a_hi)

    def stage(b):
        _stage_history(full[b], hist_ref, carry_ref, tail_ref, c - 1, first, hist_rows, tm)

    def finish(b):
        def store(r0, rows, acc):
            c_ref[pl.ds(r0, rows), :] = acc + cb_ref[...]

        if sample:
            _causal_taps_banded(full[b], cw_ref, hist_rows, tm, shift, store)
        else:
            _causal_taps_rowshift(full[b], cw_ref, shifted_ref, hist_rows, tm, store)

    _chunk_pipeline(c, n_c, produce, stage, finish, prologue=prologue)


def _glu_conv(h, w1, b1, conv_w, conv_b, layer, hist, *, tm, tc, shift, tiles_per_seq):
    n, d = h.shape
    _, kw, d_c = conv_w.shape
    n_c = d_c // tc
    n_i = n // tm
    sample = hist is not None
    hist_rows = (kw - 1) * shift if sample else -(-(kw - 1) // V7X_SUBLANES) * V7X_SUBLANES
    cur = lambda c: jnp.minimum(c, n_c - 1)
    in_specs = [
        pl.BlockSpec((tm, d), lambda i, c: (i, 0)),
        pl.BlockSpec((None, d, tc), lambda i, c: (layer, 0, cur(c))),
        pl.BlockSpec((None, d, tc), lambda i, c: (layer, 0, n_c + cur(c))),
        pl.BlockSpec((None, 1, tc), lambda i, c: (layer, 0, cur(c))),
        pl.BlockSpec((None, 1, tc), lambda i, c: (layer, 0, n_c + cur(c))),
        pl.BlockSpec((None, kw, tc), lambda i, c: (layer, 0, _prev(c))),
        pl.BlockSpec((None, 1, tc), lambda i, c: (layer, 0, _prev(c))),
    ]
    b1r = b1.reshape(b1.shape[0], 1, 2 * d_c)
    args = [h, w1, w1, b1r, b1r, conv_w, conv_b.reshape(conv_b.shape[0], 1, d_c)]
    scratch = [pltpu.VMEM((hist_rows + tm, tc), F32), pltpu.VMEM((hist_rows + tm, tc), F32)]
    if sample:
        in_specs.append(pl.BlockSpec((hist_rows, tc), lambda i, c: (0, _prev(c))))
        args.append(hist)
        tail_shape = jax.ShapeDtypeStruct((hist_rows, d_c), F32)
        tail_spec = pl.BlockSpec((hist_rows, tc), lambda i, c: (0, _prev(c)))
    else:
        scratch.append(pltpu.VMEM((n_c, hist_rows, tc), F32))
        scratch.append(pltpu.VMEM((V7X_SUBLANES - 1, tm + hist_rows - V7X_SUBLANES, tc), F32))
        tail_shape = jax.ShapeDtypeStruct((n_i, hist_rows, d_c), F32)
        tail_spec = pl.BlockSpec((None, hist_rows, tc), lambda i, c: (i, 0, _prev(c)))
    return pl.pallas_call(
        functools.partial(_glu_conv_body, tm=tm, n_c=n_c, hist_rows=hist_rows, shift=shift,
                          tiles_per_seq=tiles_per_seq, sample=sample),
        grid=(n_i, n_c + 1),
        in_specs=in_specs,
        out_specs=[pl.BlockSpec((tm, tc), lambda i, c: (i, _prev(c))), tail_spec],
        out_shape=[jax.ShapeDtypeStruct((n, d_c), F32), tail_shape],
        scratch_shapes=scratch,
        compiler_params=_cparams(("arbitrary", "arbitrary")),
        name="glu_conv",
    )(*args)


def _qkv_body(h_ref, w_ref, qg_ref, kg_ref, cos_ref, sin_ref,
              qn_ref, ckv_ref, kpe_ref, kf_ref, *, q_lora, kv_lora):
    z = _dot(h_ref[...], w_ref[...])
    qn_ref[...] = _rms_rows(z[:, :q_lora], qg_ref[...]).astype(qn_ref.dtype)
    ckv = _rms_rows(z[:, q_lora:q_lora + kv_lora], kg_ref[...])
    o = q_lora + kv_lora
    kpe = (z[:, o:o + V7X_LANES] * cos_ref[...]
           + z[:, o + V7X_LANES:o + 2 * V7X_LANES] * sin_ref[...])
    ckv_ref[...] = ckv
    kpe_ref[...] = kpe
    kf_ref[:, :kv_lora] = ckv.astype(kf_ref.dtype)
    kf_ref[:, kv_lora:] = kpe.astype(kf_ref.dtype)


def _qkv_proj(h, w_qkv, q_gain, kv_gain, cos_t, sin_t, *, tm, rope_tiles):
    n, d = h.shape
    q_lora = q_gain.shape[0]
    kv_lora = kv_gain.shape[0]
    wn = w_qkv.shape[1]
    kfw = kv_lora + V7X_LANES
    return pl.pallas_call(
        functools.partial(_qkv_body, q_lora=q_lora, kv_lora=kv_lora),
        grid=(n // tm,),
        in_specs=[
            pl.BlockSpec((tm, d), lambda i: (i, 0)),
            pl.BlockSpec((d, wn), lambda i: (0, 0)),
            pl.BlockSpec((1, q_lora), lambda i: (0, 0)),
            pl.BlockSpec((1, kv_lora), lambda i: (0, 0)),
            pl.BlockSpec((tm, V7X_LANES), lambda i: (i % rope_tiles, 0)),
            pl.BlockSpec((tm, V7X_LANES), lambda i: (i % rope_tiles, 0)),
        ],
        out_specs=[
            pl.BlockSpec((tm, q_lora), lambda i: (i, 0)),
            pl.BlockSpec((tm, kv_lora), lambda i: (i, 0)),
            pl.BlockSpec((tm, V7X_LANES), lambda i: (i, 0)),
            pl.BlockSpec((tm, kfw), lambda i: (i, 0)),
        ],
        out_shape=[
            jax.ShapeDtypeStruct((n, q_lora), BF16),
            jax.ShapeDtypeStruct((n, kv_lora), F32),
            jax.ShapeDtypeStruct((n, V7X_LANES), F32),
            jax.ShapeDtypeStruct((n, kfw), BF16),
        ],
        compiler_params=_cparams(("arbitrary",)),
        name="qkv_proj",
    )(h, w_qkv, q_gain.reshape(1, q_lora), kv_gain.reshape(1, kv_lora), cos_t, sin_t)


def _q_up_body(qn_ref, wn_ref, wp_ref, wr_ref, uk_ref, cos_ref, sin_ref, o_ref, *, hp, nope, kv_lora):
    qn = qn_ref[...]
    q_nope = _dot(qn, wn_ref[...]).astype(BF16)
    pe = _dot(qn, wp_ref[...])
    rot = _dot(qn, wr_ref[...])
    cos = cos_ref[...]
    sin = sin_ref[...]
    for hh in range(hp):
        lat = _dot(q_nope[:, hh * nope:(hh + 1) * nope], uk_ref[hh])
        sl = slice(hh * V7X_LANES, (hh + 1) * V7X_LANES)
        o_ref[hh, :, :kv_lora] = lat.astype(o_ref.dtype)
        o_ref[hh, :, kv_lora:] = (pe[:, sl] * cos + rot[:, sl] * sin).astype(o_ref.dtype)


def _q_up(qn, w_nope, w_pe, w_rot, w_ukt, cos_t, sin_t, *, tm, rope_tiles, hp):
    n, q_lora = qn.shape
    n_heads, nope, kv_lora = w_ukt.shape
    qw = kv_lora + V7X_LANES
    return pl.pallas_call(
        functools.partial(_q_up_body, hp=hp, nope=nope, kv_lora=kv_lora),
        grid=(n // tm, n_heads // hp),
        in_specs=[
            pl.BlockSpec((tm, q_lora), lambda i, g: (i, 0)),
            pl.BlockSpec((q_lora, hp * nope), lambda i, g: (0, g)),
            pl.BlockSpec((q_lora, hp * V7X_LANES), lambda i, g: (0, g)),
            pl.BlockSpec((q_lora, hp * V7X_LANES), lambda i, g: (0, g)),
            pl.BlockSpec((hp, nope, kv_lora), lambda i, g: (g, 0, 0)),
            pl.BlockSpec((tm, V7X_LANES), lambda i, g: (i % rope_tiles, 0)),
            pl.BlockSpec((tm, V7X_LANES), lambda i, g: (i % rope_tiles, 0)),
        ],
        out_specs=pl.BlockSpec((hp, tm, qw), lambda i, g: (g, i, 0)),
        out_shape=jax.ShapeDtypeStruct((n_heads, n, qw), BF16),
        compiler_params=_cparams(("arbitrary", "arbitrary")),
        name="q_up",
    )(qn, w_nope, w_pe, w_rot, w_ukt, cos_t, sin_t)


def _softmax_update(s, v, m_ref, l_ref, acc_ref):
    m_prev = m_ref[...]
    m_new = jnp.maximum(m_prev, jnp.max(s, axis=-1, keepdims=True))
    alpha = jnp.exp(m_prev - m_new)
    p = jnp.exp(s - m_new)
    l_ref[...] = alpha * l_ref[...] + jnp.sum(p, axis=-1, keepdims=True)
    acc_ref[...] = alpha * acc_ref[...] + _dot(p.astype(BF16), v)
    m_ref[...] = m_new


def _attn_prompt_body(q_ref, k_ref, wv_ref, o_ref, m_ref, l_ref, acc_ref,
                      *, tq, tk, n_k, n_heads, kv_lora, v_dim, scale):
    qi = pl.program_id(1)
    ki = pl.program_id(2)
    first_masked = (qi * tq) // tk
    last = ((qi + 1) * tq - 1) // tk
    rows = n_heads * tq

    @pl.when(ki == 0)
    def _():
        m_ref[...] = jnp.full(m_ref.shape, -jnp.inf, F32)
        l_ref[...] = jnp.zeros(l_ref.shape, F32)
        acc_ref[...] = jnp.zeros(acc_ref.shape, F32)

    def scores():
        q = q_ref[...].reshape(rows, q_ref.shape[-1])
        return _dot_nt(q, k_ref[...]) * scale

    @pl.when(ki < first_masked)
    def _():
        _softmax_update(scores(), k_ref[:, :kv_lora], m_ref, l_ref, acc_ref)

    @pl.when(jnp.logical_and(ki >= first_masked, ki <= last))
    def _():
        s = scores()
        q_pos = qi * tq + lax.rem(lax.broadcasted_iota(jnp.int32, (rows, tk), 0), tq)
        k_pos = ki * tk + lax.broadcasted_iota(jnp.int32, (rows, tk), 1)
        s = jnp.where(k_pos <= q_pos, s, -jnp.inf)
        _softmax_update(s, k_ref[:, :kv_lora], m_ref, l_ref, acc_ref)

    @pl.when(ki == n_k - 1)
    def _():
        o_lat = (acc_ref[...] / l_ref[...]).astype(BF16)
        for hh in range(n_heads):
            o_ref[:, hh * v_dim:(hh + 1) * v_dim] = _dot(
                o_lat[hh * tq:(hh + 1) * tq], wv_ref[hh]).astype(o_ref.dtype)


def _attn_prompt(q, kf, w_uvh, *, batch, seq, tq, tk, scale):
    n_heads, n, qw = q.shape
    _, kv_lora, v_dim = w_uvh.shape
    n_q = seq // tq
    n_k = seq // tk
    rows = n_heads * tq

    def k_map(b, qi, ki):
        last = ((qi + 1) * tq - 1) // tk
        return (b * n_k + jnp.minimum(ki, last), 0)

    return pl.pallas_call(
        functools.partial(_attn_prompt_body, tq=tq, tk=tk, n_k=n_k, n_heads=n_heads,
                          kv_lora=kv_lora, v_dim=v_dim, scale=scale),
        grid=(batch, n_q, n_k),
        in_specs=[
            pl.BlockSpec((n_heads, tq, qw), lambda b, qi, ki: (0, b * n_q + qi, 0)),
            pl.BlockSpec((tk, qw), k_map),
            pl.BlockSpec((n_heads, kv_lora, v_dim), lambda b, qi, ki: (0, 0, 0)),
        ],
        out_specs=pl.BlockSpec((tq, n_heads * v_dim), lambda b, qi, ki: (b * n_q + qi, 0)),
        out_shape=jax.ShapeDtypeStruct((n, n_heads * v_dim), BF16),
        scratch_shapes=[pltpu.VMEM((rows, 1), F32), pltpu.VMEM((rows, 1), F32),
                        pltpu.VMEM((rows, kv_lora), F32)],
        compiler_params=_cparams(("arbitrary", "arbitrary", "arbitrary")),
        name="attn_prompt",
    )(q, kf, w_uvh)


def _attn_sample_body(pt_ref, q_ref, kn_ref, ckv_hbm, kpet_hbm, o_ref,
                      cbuf, pbuf, sem, m_ref, l_ref, acc_ref,
                      *, layer, pages, page, n_half, n_heads, n_new, kv_lora, rope, scale):
    b = pl.program_id(0)
    hf = pl.program_id(1)
    step = b * n_half + hf
    n_steps = pl.num_programs(0) * n_half
    slot = lax.rem(step, 2)

    def page_copies(bb, hh, sl):
        cps = []
        for p in range(pages):
            pg = pt_ref[bb, hh * pages + p]
            keys = pl.ds(p * page, page)
            cps.append(pltpu.make_async_copy(ckv_hbm.at[layer, pg], cbuf.at[sl, keys, :], sem.at[0, sl]))
            cps.append(pltpu.make_async_copy(kpet_hbm.at[layer, pg], pbuf.at[sl, :, keys], sem.at[1, sl]))
        return cps

    @pl.when(step == 0)
    def _():
        for cp in page_copies(b, hf, slot):
            cp.start()

    @pl.when(step + 1 < n_steps)
    def _():
        wrap = hf == n_half - 1
        nb = jnp.where(wrap, b + 1, b)
        nh = jnp.where(wrap, 0, hf + 1)
        for cp in page_copies(nb, nh, 1 - slot):
            cp.start()

    for cp in page_copies(b, hf, slot):
        cp.wait()

    @pl.when(hf == 0)
    def _():
        m_ref[...] = jnp.full(m_ref.shape, -jnp.inf, F32)
        l_ref[...] = jnp.zeros(l_ref.shape, F32)
        acc_ref[...] = jnp.zeros(acc_ref.shape, F32)

    q = q_ref[0]
    kc = cbuf[slot].astype(BF16)
    kpt = pbuf[slot].astype(BF16)
    s = (_dot_nt(q[:, :kv_lora], kc) + _dot(q[:, kv_lora:kv_lora + rope], kpt)) * scale
    _softmax_update(s, kc, m_ref, l_ref, acc_ref)

    @pl.when(hf == n_half - 1)
    def _():
        kn = kn_ref[0]
        sn = _dot_nt(q, kn) * scale
        t_row = lax.broadcasted_iota(jnp.int32, sn.shape, 0) // n_heads
        u_col = lax.broadcasted_iota(jnp.int32, sn.shape, 1)
        sn = jnp.where(jnp.logical_and(u_col <= t_row, u_col < n_new), sn, -jnp.inf)
        _softmax_update(sn, kn[:, :kv_lora], m_ref, l_ref, acc_ref)
        o_ref[0] = (acc_ref[...] / l_ref[...]).astype(o_ref.dtype)


def _attn_sample(page_table, q, k_new, cache_ckv, cache_kpe_t, *, layer, n_new, n_heads, scale):
    bd, rows, qw = q.shape
    _, _, page, kv_lora = cache_ckv.shape
    rope = cache_kpe_t.shape[2]
    n_pages = page_table.shape[1]
    past = n_pages * page
    keys = min(SAMPLE_KEYS_PER_STEP, past)
    n_half = past // keys
    pages = keys // page
    grid_spec = pltpu.PrefetchScalarGridSpec(
        num_scalar_prefetch=1,
        grid=(bd, n_half),
        in_specs=[
            pl.BlockSpec((1, rows, qw), lambda b, hf, pt: (b, 0, 0)),
            pl.BlockSpec((1, k_new.shape[1], qw), lambda b, hf, pt: (b, 0, 0)),
            pl.BlockSpec(memory_space=pl.ANY),
            pl.BlockSpec(memory_space=pl.ANY),
        ],
        out_specs=pl.BlockSpec((1, rows, kv_lora), lambda b, hf, pt: (b, 0, 0)),
        scratch_shapes=[
            pltpu.VMEM((2, keys, kv_lora), F32),
            pltpu.VMEM((2, rope, keys), F32),
            pltpu.SemaphoreType.DMA((2, 2)),
            pltpu.VMEM((rows, 1), F32),
            pltpu.VMEM((rows, 1), F32),
            pltpu.VMEM((rows, kv_lora), F32),
        ],
    )
    return pl.pallas_call(
        functools.partial(_attn_sample_body, layer=layer, pages=pages, page=page, n_half=n_half,
                          n_heads=n_heads, n_new=n_new, kv_lora=kv_lora, rope=rope, scale=scale),
        grid_spec=grid_spec,
        out_shape=jax.ShapeDtypeStruct((bd, rows, kv_lora), BF16),
        compiler_params=_cparams(("arbitrary", "arbitrary")),
        name="attn_sample",
    )(page_table, q, k_new, cache_ckv, cache_kpe_t)


def _head_proj_body(x_ref, w_ref, o_ref):
    o_ref[...] = _dot(x_ref[0], w_ref[0]).astype(o_ref.dtype)


def _head_proj(o_lat, w_uvh):
    n_heads, n, kv_lora = o_lat.shape
    v_dim = w_uvh.shape[-1]
    return pl.pallas_call(
        _head_proj_body,
        grid=(n_heads,),
        in_specs=[pl.BlockSpec((1, n, kv_lora), lambda h: (h, 0, 0)),
                  pl.BlockSpec((1, kv_lora, v_dim), lambda h: (h, 0, 0))],
        out_specs=pl.BlockSpec((n, v_dim), lambda h: (0, h)),
        out_shape=jax.ShapeDtypeStruct((n, n_heads * v_dim), BF16),
        compiler_params=_cparams(("arbitrary",)),
        name="head_proj",
    )(o_lat, w_uvh)


def _rope_tables(pos, rope):
    half = rope // 2
    inv = 1.0 / (ROPE_THETA ** (jnp.arange(half, dtype=F32) * (2.0 / rope)))
    ang = pos.astype(F32)[:, None] * inv[None, :]
    cos = jnp.cos(ang)
    sin = jnp.sin(ang)
    pad = jnp.zeros((pos.shape[0], V7X_LANES - rope), F32)
    return (jnp.concatenate([cos, cos, pad], axis=-1), jnp.concatenate([-sin, sin, pad], axis=-1))


def _swap_halves(w):
    half = w.shape[-1] // 2
    return jnp.concatenate([w[..., half:], w[..., :half]], axis=-1)


def _pad_lanes(w):
    pad = [(0, 0)] * (w.ndim - 1) + [(0, V7X_LANES - w.shape[-1])]
    return jnp.pad(w, pad)


def _time_major(a):
    b, t, c = a.shape
    return jnp.swapaxes(a, 0, 1).reshape(t * b, c)


def _batch_major(a, b):
    tb, c = a.shape
    return jnp.swapaxes(a.reshape(tb // b, b, c), 0, 1)


def kernel(x_prompt, x_sample, cache_ckv, cache_kpe, page_table, state_sconv, state_conf, state_ffn,
           norm_mix, norm_ffn, norm_final, w_in_e, conv_a, q_norm, w_uq, kv_norm, w_ukv, w_out_e,
           w_pw1, b_pw1, conv_c, conv_c_b, ln_c_g, ln_c_b, w_pw2, b_pw2, w_up, conv_f, w_down):
    bp, seq, d = x_prompt.shape
    bd, t_len, _ = x_sample.shape
    depth = norm_mix.shape[0]
    d_a = conv_a.shape[-1]
    q_lora = q_norm.shape[-1]
    kv_lora = kv_norm.shape[-1]
    rope = cache_kpe.shape[-1]
    n_heads = w_ukv.shape[2]
    nope = w_uq.shape[-1] // n_heads - rope
    v_dim = w_ukv.shape[-1] - nope
    page = cache_ckv.shape[2]
    past_len = page_table.shape[1] * page
    scale = float(nope + rope) ** -0.5
    n_p = bp * seq
    n_s = bd * t_len

    tm_p = min(TOKEN_TILE, seq)
    tps = seq // tm_p
    tq = min(ATTN_Q_TILE, seq)
    tk = min(ATTN_K_TILE, seq)
    tf = min(FF_TILE, w_down.shape[1])
    tc_a = min(CH_TILE, d_a)
    tc_c = min(CH_TILE, d)
    tn = min(OUT_TILE, d)

    cos_p, sin_p = _rope_tables(jnp.arange(seq), rope)
    cos_s, sin_s = _rope_tables(past_len + jnp.repeat(jnp.arange(t_len), bd), rope)

    tm_q = Q_UP_TILE if seq % Q_UP_TILE == 0 else tm_p
    groups = (
        dict(n=n_p, tm=tm_p, shift=1, tps=tps, cos=cos_p, sin=sin_p, rope_tiles=tps,
             tm_q=tm_q, rope_tiles_q=seq // tm_q),
        dict(n=n_s, tm=n_s, shift=bd, tps=1, cos=cos_s, sin=sin_s, rope_tiles=1,
             tm_q=n_s, rope_tiles_q=1),
    )
    xs = [x_prompt.reshape(n_p, d), _time_major(x_sample)]
    hs = [_rmsnorm(x, norm_mix[0], g["tm"]) for x, g in zip(xs, groups)]

    w_in_b = w_in_e.astype(BF16)
    w_out_b = w_out_e.astype(BF16)
    w_pw1_b = w_pw1.astype(BF16)
    w_pw2_b = w_pw2.astype(BF16)
    w_up_b = w_up.astype(BF16)
    w_down_b = w_down.astype(BF16)
    cache_kpe_t = jnp.swapaxes(cache_kpe, 2, 3)
    zero_bias = jnp.zeros((d,), F32)

    sc_new, cf_new, ff_new, ckvs, kpes = [[], []], [[], []], [[], []], [[], []], [[], []]

    def prompt_tail(tail, rows):
        return tail[tps - 1::tps, tail.shape[1] - rows:, :]

    for l in range(depth):
        nxt_gain = norm_ffn[l]
        if l % 2 == 0:
            e = l // 2
            w_in = w_in_e[e]
            o = 3 * d_a
            w_kr = w_in[:, o + q_lora + kv_lora:]
            w_qkv = jnp.concatenate(
                [w_in[:, o:o + q_lora + kv_lora], _pad_lanes(w_kr), _pad_lanes(_swap_halves(w_kr))],
                axis=-1).astype(BF16)
            wq = w_uq[e].reshape(q_lora, n_heads, nope + rope)
            w_nope = wq[..., :nope].reshape(q_lora, n_heads * nope).astype(BF16)
            w_pe = _pad_lanes(wq[..., nope:]).reshape(q_lora, n_heads * V7X_LANES).astype(BF16)
            w_rot = _pad_lanes(_swap_halves(wq[..., nope:])).reshape(q_lora, n_heads * V7X_LANES).astype(BF16)
            w_ukt = jnp.transpose(w_ukv[e][..., :nope], (1, 2, 0)).astype(BF16)
            w_uvh = jnp.transpose(w_ukv[e][..., nope:], (1, 0, 2)).astype(BF16)
            for gi, g in enumerate(groups):
                hist = None if gi == 0 else _time_major(state_sconv[e])
                ya, tail = _short_conv(hs[gi], w_in_b, conv_a, e, hist, d_a=d_a, tm=g["tm"], tc=tc_a,
                                       shift=g["shift"], tiles_per_seq=g["tps"])
                qn, ckv, kpe, kf = _qkv_proj(hs[gi], w_qkv, q_norm[e], kv_norm[e], g["cos"], g["sin"],
                                             tm=g["tm"], rope_tiles=g["rope_tiles"])
                qf = _q_up(qn, w_nope, w_pe, w_rot, w_ukt, g["cos"], g["sin"],
                           tm=g["tm_q"], rope_tiles=g["rope_tiles_q"], hp=2)
                if gi == 0:
                    o_mix = _attn_prompt(qf, kf, w_uvh, batch=bp, seq=seq, tq=tq, tk=tk, scale=scale)
                    sc_new[gi].append(prompt_tail(tail, 2))
                    ckvs[gi].append(ckv.reshape(bp, seq, kv_lora))
                    kpes[gi].append(kpe[:, :rope].reshape(bp, seq, rope))
                else:
                    qw = qf.shape[-1]
                    q_seq = jnp.transpose(qf.reshape(n_heads, t_len, bd, qw), (2, 1, 0, 3))
                    q_seq = q_seq.reshape(bd, t_len * n_heads, qw)
                    k_new = jnp.pad(_batch_major(kf, bd), ((0, 0), (0, 2 * V7X_SUBLANES - t_len), (0, 0)))
                    o_lat = _attn_sample(page_table, q_seq, k_new, cache_ckv, cache_kpe_t, layer=e,
                                         n_new=t_len, n_heads=n_heads, scale=scale)
                    o_lat = jnp.transpose(o_lat.reshape(bd, t_len, n_heads, kv_lora), (2, 1, 0, 3))
                    o_mix = _head_proj(o_lat.reshape(n_heads, n_s, kv_lora), w_uvh)
                    sc_new[gi].append(_batch_major(tail, bd))
                    ckvs[gi].append(_batch_major(ckv, bd))
                    kpes[gi].append(_batch_major(kpe[:, :rope], bd))
                xs[gi], hs[gi] = _linear_residual_norm(
                    [ya, o_mix], w_out_b, e, zero_bias, xs[gi], nxt_gain, tm=g["tm"], tn=tn)
        else:
            o = l // 2
            kw = conv_c.shape[1]
            for gi, g in enumerate(groups):
                hist = None if gi == 0 else _time_major(state_conf[o])
                cpre, tail = _glu_conv(hs[gi], w_pw1_b, b_pw1, conv_c, conv_c_b, o, hist,
                                       tm=g["tm"], tc=tc_c, shift=g["shift"], tiles_per_seq=g["tps"])
                act = _ln_silu(cpre, ln_c_g[o], ln_c_b[o], g["tm"])
                cf_new[gi].append(prompt_tail(tail, kw - 1) if gi == 0 else _batch_major(tail, bd))
                xs[gi], hs[gi] = _linear_residual_norm(
                    [act], w_pw2_b, o, b_pw2[o], xs[gi], nxt_gain, tm=g["tm"], tn=tn)
        final = l == depth - 1
        gain = norm_final if final else norm_mix[l + 1]
        for gi, g in enumerate(groups):
            hist = None if gi == 0 else _time_major(state_ffn[l])
            xs[gi], hs[gi], tail_g, tail_u = _conv_ffn(
                hs[gi], xs[gi], w_up_b, conv_f, w_down_b, l, gain, hist, tm=g["tm"], tf=tf,
                shift=g["shift"], tiles_per_seq=g["tps"], out_dtype=F32 if final else BF16)
            if gi == 0:
                ff_new[gi].append(jnp.concatenate([prompt_tail(tail_g, 2), prompt_tail(tail_u, 2)], axis=-1))
            else:
                ff_new[gi].append(_batch_major(jnp.concatenate([tail_g, tail_u], axis=-1), bd))

    y_prompt = hs[0].reshape(bp, seq, d)
    y_sample = _batch_major(hs[1], bd)
    st = lambda parts: jnp.stack(parts)
    return (y_prompt, y_sample, st(ckvs[0]), st(kpes[0]), st(ckvs[1]), st(kpes[1]),
            st(sc_new[0]), st(sc_new[1]), st(cf_new[0]), st(cf_new[1]), st(ff_new[0]), st(ff_new[1]))
```

```python
import functools

import jax
import jax.numpy as jnp
from jax import lax
from jax.experimental import pallas as pl
from jax.experimental.pallas import tpu as pltpu

F32 = jnp.float32
BF16 = jnp.bfloat16
EPS = 1e-6
ROPE_THETA = 10000.0

V7X_LANES = 128
V7X_SUBLANES = 8
V7X_VMEM_BYTES = 64 * 1024 * 1024
VMEM_LIMIT_BYTES = V7X_VMEM_BYTES - 6 * 1024 * 1024

TOKEN_TILE = 512
FF_TILE = 256
CH_TILE = 256
OUT_TILE = 1024
Q_UP_TILE = 1024
ATTN_Q_TILE = 256
ATTN_K_TILE = 512
PROMPT_HIST = V7X_SUBLANES
SAMPLE_KEYS_PER_STEP = 4096
ACC_COLS = 512
NORM_ROWS = 128
CONV_ROWS = 64


def _cparams(sem):
    return pltpu.CompilerParams(dimension_semantics=sem, vmem_limit_bytes=VMEM_LIMIT_BYTES)


def _sigmoid(x):
    return 1.0 / (1.0 + jnp.exp(-x))


def _dot(a, b):
    return jnp.dot(a, b, preferred_element_type=F32)


def _dot_nt(a, b):
    return lax.dot_general(a, b, (((1,), (1,)), ((), ())), preferred_element_type=F32)


def _rms_rows(x, gain):
    return x * lax.rsqrt(jnp.mean(x * x, axis=-1, keepdims=True) + EPS) * gain


def _accumulate_dot(acc_ref, a, w_ref):
    d = acc_ref.shape[1]
    step = min(d, ACC_COLS)
    for n0 in range(0, d, step):
        cols = slice(n0, n0 + step)
        acc_ref[:, cols] += _dot(a, w_ref[:, cols])


def _store_rms(dst_ref, src_ref, gain_ref):
    tm = src_ref.shape[0]
    step = min(tm, NORM_ROWS)
    for r0 in range(0, tm, step):
        rows = pl.ds(r0, step)
        dst_ref[rows, :] = _rms_rows(src_ref[rows, :], gain_ref[...]).astype(dst_ref.dtype)


def _causal_taps(full_ref, w_ref, hist_rows, tm, shift):
    kw = w_ref.shape[0]
    acc = None
    for k in range(kw):
        off = hist_rows - (kw - 1 - k) * shift
        term = full_ref[pl.ds(off, tm), :] * w_ref[pl.ds(k, 1), :]
        acc = term if acc is None else acc + term
    return acc


def _causal_taps_banded(full_ref, w_ref, hist_rows, tm, shift, store):
    kw = w_ref.shape[0]
    band = min(tm, CONV_ROWS)
    for r0 in range(0, tm, band):
        acc = None
        for k in range(kw):
            off = hist_rows - (kw - 1 - k) * shift
            term = full_ref[pl.ds(off + r0, band), :] * w_ref[pl.ds(k, 1), :]
            acc = term if acc is None else acc + term
        store(r0, band, acc)


def _causal_taps_rowshift(full_ref, w_ref, shifted_ref, hist_rows, tm, store):
    kw = w_ref.shape[0]
    span = shifted_ref.shape[1]
    for r in range(1, V7X_SUBLANES):
        shifted_ref[r - 1] = full_ref[pl.ds(r, span), :]
    band = min(tm, CONV_ROWS)
    for r0 in range(0, tm, band):
        acc = None
        for k in range(kw):
            off = hist_rows - (kw - 1 - k)
            r = off % V7X_SUBLANES
            if r == 0:
                src = full_ref[pl.ds(off + r0, band), :]
            else:
                src = shifted_ref[r - 1, pl.ds(off - r + r0, band), :]
            term = src * w_ref[pl.ds(k, 1), :]
            acc = term if acc is None else acc + term
        store(r0, band, acc)


def _stage_history(full_ref, hist_ref, carry_ref, tail_ref, chunk, first_of_seq, hist_rows, tm):
    if hist_ref is not None:
        full_ref[pl.ds(0, hist_rows), :] = hist_ref[...]
        tail_rows = tail_ref.shape[-2]
        tail_ref[...] = full_ref[pl.ds(hist_rows + tm - tail_rows, tail_rows), :]
        return
    prev = carry_ref[chunk]
    full_ref[pl.ds(0, hist_rows), :] = jnp.where(first_of_seq, jnp.zeros_like(prev), prev)
    tail = full_ref[pl.ds(tm, hist_rows), :]
    carry_ref[chunk] = tail
    tail_ref[...] = tail


def _chunk_pipeline(c, n_c, produce, stage, finish, prologue=None, after_first=None, epilogue=None):
    even = lax.rem(c, 2) == 0
    mid = jnp.logical_and(c > 0, c < n_c)

    @pl.when(c == 0)
    def _():
        if prologue is not None:
            prologue()
        produce(0)
        if after_first is not None:
            after_first()

    @pl.when(jnp.logical_and(mid, even))
    def _():
        stage(1)
        produce(0)
        finish(1)

    @pl.when(jnp.logical_and(mid, jnp.logical_not(even)))
    def _():
        stage(0)
        produce(1)
        finish(0)

    @pl.when(c == n_c)
    def _():
        last = (n_c - 1) % 2
        stage(last)
        finish(last)
        if epilogue is not None:
            epilogue()


def _zero_carries_once(i, carry_refs):
    @pl.when(i == 0)
    def _():
        for ref in carry_refs:
            ref[...] = jnp.zeros(ref.shape, F32)


def _prev(c):
    return jnp.maximum(c - 1, 0)


def _rms_body(x_ref, g_ref, o_ref):
    _store_rms(o_ref, x_ref, g_ref)


def _rmsnorm(x, gain, tm):
    n, d = x.shape
    return pl.pallas_call(
        _rms_body,
        grid=(n // tm,),
        in_specs=[pl.BlockSpec((tm, d), lambda i: (i, 0)), pl.BlockSpec((1, d), lambda i: (0, 0))],
        out_specs=pl.BlockSpec((tm, d), lambda i: (i, 0)),
        out_shape=jax.ShapeDtypeStruct((n, d), BF16),
        compiler_params=_cparams(("arbitrary",)),
        name="rmsnorm",
    )(x, gain.reshape(1, d))


def _ln_silu_body(c_ref, g_ref, b_ref, o_ref):
    tm = c_ref.shape[0]
    step = min(tm, NORM_ROWS)
    for r0 in range(0, tm, step):
        rows = pl.ds(r0, step)
        c = c_ref[rows, :]
        mu = jnp.mean(c, axis=-1, keepdims=True)
        xc = c - mu
        y = xc * lax.rsqrt(jnp.mean(xc * xc, axis=-1, keepdims=True) + EPS)
        y = y * g_ref[...] + b_ref[...]
        o_ref[rows, :] = (y * _sigmoid(y)).astype(o_ref.dtype)


def _ln_silu(c, gain, bias, tm):
    n, d = c.shape
    return pl.pallas_call(
        _ln_silu_body,
        grid=(n // tm,),
        in_specs=[pl.BlockSpec((tm, d), lambda i: (i, 0)),
                  pl.BlockSpec((1, d), lambda i: (0, 0)),
                  pl.BlockSpec((1, d), lambda i: (0, 0))],
        out_specs=pl.BlockSpec((tm, d), lambda i: (i, 0)),
        out_shape=jax.ShapeDtypeStruct((n, d), BF16),
        compiler_params=_cparams(("arbitrary",)),
        name="ln_silu",
    )(c, gain.reshape(1, d), bias.reshape(1, d))


def _lrn_body(*refs, n_a, n_n, tn):
    a_refs = refs[:n_a]
    w_refs = refs[n_a:2 * n_a]
    b_ref, x_ref, gn_ref, xo_ref, hn_ref, xs_ref = refs[2 * n_a:]
    n = pl.program_id(1)
    y = x_ref[...] + b_ref[...]
    for a_ref, w_ref in zip(a_refs, w_refs):
        y = y + _dot(a_ref[...], w_ref[...])
    xo_ref[...] = y
    xs_ref[n] = y

    @pl.when(n == n_n - 1)
    def _():
        tm = xs_ref.shape[1]
        d = n_n * tn
        step = min(tm, NORM_ROWS)
        for r0 in range(0, tm, step):
            rows = pl.ds(r0, step)
            ssq = None
            for cc in range(n_n):
                v = xs_ref[cc, rows, :]
                part = jnp.sum(v * v, axis=-1, keepdims=True)
                ssq = part if ssq is None else ssq + part
            r = lax.rsqrt(ssq * (1.0 / d) + EPS)
            for cc in range(n_n):
                cols = slice(cc * tn, (cc + 1) * tn)
                hn_ref[rows, cols] = (xs_ref[cc, rows, :] * r * gn_ref[:, cols]).astype(hn_ref.dtype)


def _linear_residual_norm(a_list, w, layer, bias, x, gain, tm, tn):
    n, d = x.shape
    n_a = len(a_list)
    k_each = a_list[0].shape[1]
    n_n = d // tn
    a_specs = [pl.BlockSpec((tm, k_each), lambda i, nn: (i, 0)) for _ in range(n_a)]
    w_specs = [pl.BlockSpec((None, k_each, tn), lambda i, nn, p=p: (layer, p, nn)) for p in range(n_a)]
    return pl.pallas_call(
        functools.partial(_lrn_body, n_a=n_a, n_n=n_n, tn=tn),
        grid=(n // tm, n_n),
        in_specs=a_specs + w_specs + [
            pl.BlockSpec((1, tn), lambda i, nn: (0, nn)),
            pl.BlockSpec((tm, tn), lambda i, nn: (i, nn)),
            pl.BlockSpec((1, d), lambda i, nn: (0, 0)),
        ],
        out_specs=[pl.BlockSpec((tm, tn), lambda i, nn: (i, nn)),
                   pl.BlockSpec((tm, d), lambda i, nn: (i, 0))],
        out_shape=[jax.ShapeDtypeStruct((n, d), F32), jax.ShapeDtypeStruct((n, d), BF16)],
        scratch_shapes=[pltpu.VMEM((n_n, tm, tn), F32)],
        compiler_params=_cparams(("arbitrary", "arbitrary")),
        name="linear_residual_norm",
    )(*a_list, *([w] * n_a), bias.reshape(1, d), x, gain.reshape(1, d))


def _ffn_body(*refs, tm, n_j, hist_rows, shift, tiles_per_seq, sample):
    if sample:
        (h_ref, x_hbm, wg_ref, wu_ref, cg_ref, cu_ref, wd_ref, gn_ref, hg_ref, hu_ref,
         xo_ref, hn_ref, tg_ref, tu_ref, fg0, fg1, fu0, fu1, sem) = refs
        cag_ref = cau_ref = None
    else:
        (h_ref, x_hbm, wg_ref, wu_ref, cg_ref, cu_ref, wd_ref, gn_ref,
         xo_ref, hn_ref, tg_ref, tu_ref, fg0, fg1, fu0, fu1, cag_ref, cau_ref, sem) = refs
        hg_ref = hu_ref = None
    fg = (fg0, fg1)
    fu = (fu0, fu1)
    i = pl.program_id(0)
    j = pl.program_id(1)
    first = lax.rem(i, tiles_per_seq) == 0

    def x_copy():
        return pltpu.make_async_copy(x_hbm.at[pl.ds(i * tm, tm), :], xo_ref, sem)

    def prologue():
        x_copy().start()
        if not sample:
            _zero_carries_once(i, (cag_ref, cau_ref))

    def produce(b):
        h = h_ref[...]
        fg[b][pl.ds(hist_rows, tm), :] = _dot(h, wg_ref[...])
        fu[b][pl.ds(hist_rows, tm), :] = _dot(h, wu_ref[...])

    def stage(b):
        chunk = j - 1
        _stage_history(fg[b], hg_ref, cag_ref, tg_ref, chunk, first, hist_rows, tm)
        _stage_history(fu[b], hu_ref, cau_ref, tu_ref, chunk, first, hist_rows, tm)

    def finish(b):
        cg = _causal_taps(fg[b], cg_ref, hist_rows, tm, shift)
        cu = _causal_taps(fu[b], cu_ref, hist_rows, tm, shift)
        act = (cg * _sigmoid(cg) * cu).astype(BF16)
        _accumulate_dot(xo_ref, act, wd_ref)

    _chunk_pipeline(j, n_j, produce, stage, finish, prologue=prologue,
                    after_first=lambda: x_copy().wait(),
                    epilogue=lambda: _store_rms(hn_ref, xo_ref, gn_ref))


def _conv_ffn(h, x, w_up, conv_w, w_down, layer, gain, hist, *, tm, tf, shift, tiles_per_seq, out_dtype):
    n, d = x.shape
    d_ff = w_down.shape[1]
    n_j = d_ff // tf
    n_i = n // tm
    sample = hist is not None
    hist_rows = 2 * shift if sample else PROMPT_HIST
    cur = lambda j: jnp.minimum(j, n_j - 1)
    in_specs = [
        pl.BlockSpec((tm, d), lambda i, j: (i, 0)),
        pl.BlockSpec(memory_space=pl.ANY),
        pl.BlockSpec((None, d, tf), lambda i, j: (layer, 0, cur(j))),
        pl.BlockSpec((None, d, tf), lambda i, j: (layer, 0, n_j + cur(j))),
        pl.BlockSpec((None, 3, tf), lambda i, j: (layer, 0, _prev(j))),
        pl.BlockSpec((None, 3, tf), lambda i, j: (layer, 0, n_j + _prev(j))),
        pl.BlockSpec((None, tf, d), lambda i, j: (layer, _prev(j), 0)),
        pl.BlockSpec((1, d), lambda i, j: (0, 0)),
    ]
    args = [h, x, w_up, w_up, conv_w, conv_w, w_down, gain.reshape(1, d)]
    scratch = [pltpu.VMEM((hist_rows + tm, tf), F32) for _ in range(4)]
    if sample:
        in_specs += [pl.BlockSpec((hist_rows, tf), lambda i, j: (0, _prev(j))),
                     pl.BlockSpec((hist_rows, tf), lambda i, j: (0, n_j + _prev(j)))]
        args += [hist, hist]
        tail_shape = jax.ShapeDtypeStruct((hist_rows, d_ff), F32)
        tail_spec = pl.BlockSpec((hist_rows, tf), lambda i, j: (0, _prev(j)))
    else:
        scratch += [pltpu.VMEM((n_j, hist_rows, tf), F32), pltpu.VMEM((n_j, hist_rows, tf), F32)]
        tail_shape = jax.ShapeDtypeStruct((n_i, hist_rows, d_ff), F32)
        tail_spec = pl.BlockSpec((None, hist_rows, tf), lambda i, j: (i, 0, _prev(j)))
    scratch.append(pltpu.SemaphoreType.DMA(()))
    return pl.pallas_call(
        functools.partial(_ffn_body, tm=tm, n_j=n_j, hist_rows=hist_rows, shift=shift,
                          tiles_per_seq=tiles_per_seq, sample=sample),
        grid=(n_i, n_j + 1),
        in_specs=in_specs,
        out_specs=[pl.BlockSpec((tm, d), lambda i, j: (i, 0)),
                   pl.BlockSpec((tm, d), lambda i, j: (i, 0)),
                   tail_spec, tail_spec],
        out_shape=[jax.ShapeDtypeStruct((n, d), F32), jax.ShapeDtypeStruct((n, d), out_dtype),
                   tail_shape, tail_shape],
        scratch_shapes=scratch,
        compiler_params=_cparams(("arbitrary", "arbitrary")),
        name="conv_ffn",
    )(*args)


def _sconv_body(*refs, tm, n_c, hist_rows, shift, tiles_per_seq, sample):
    if sample:
        h_ref, wb_ref, wc_ref, wx_ref, cw_ref, hist_ref, ya_ref, tail_ref, f0, f1, z0, z1 = refs
        carry_ref = None
    else:
        h_ref, wb_ref, wc_ref, wx_ref, cw_ref, ya_ref, tail_ref, f0, f1, z0, z1, carry_ref = refs
        hist_ref = None
    full = (f0, f1)
    zb = (z0, z1)
    i = pl.program_id(0)
    c = pl.program_id(1)
    first = lax.rem(i, tiles_per_seq) == 0

    def prologue():
        if not sample:
            _zero_carries_once(i, (carry_ref,))

    def produce(b):
        h = h_ref[...]
        full[b][pl.ds(hist_rows, tm), :] = _dot(h, wc_ref[...]) * _dot(h, wx_ref[...])
        zb[b][...] = _dot(h, wb_ref[...])

    def stage(b):
        _stage_history(full[b], hist_ref, carry_ref, tail_ref, c - 1, first, hist_rows, tm)

    def finish(b):
        conv = _causal_taps(full[b], cw_ref, hist_rows, tm, shift)
        ya_ref[...] = (zb[b][...] * conv).astype(ya_ref.dtype)

    _chunk_pipeline(c, n_c, produce, stage, finish, prologue=prologue)


def _short_conv(h, w_in, conv_w, layer, hist, *, d_a, tm, tc, shift, tiles_per_seq):
    n, d = h.shape
    n_c = d_a // tc
    n_i = n // tm
    sample = hist is not None
    hist_rows = 2 * shift if sample else PROMPT_HIST
    cur = lambda c: jnp.minimum(c, n_c - 1)
    in_specs = [
        pl.BlockSpec((tm, d), lambda i, c: (i, 0)),
        pl.BlockSpec((None, d, tc), lambda i, c: (layer, 0, cur(c))),
        pl.BlockSpec((None, d, tc), lambda i, c: (layer, 0, n_c + cur(c))),
        pl.BlockSpec((None, d, tc), lambda i, c: (layer, 0, 2 * n_c + cur(c))),
        pl.BlockSpec((None, 3, tc), lambda i, c: (layer, 0, _prev(c))),
    ]
    args = [h, w_in, w_in, w_in, conv_w]
    scratch = [pltpu.VMEM((hist_rows + tm, tc), F32), pltpu.VMEM((hist_rows + tm, tc), F32),
               pltpu.VMEM((tm, tc), F32), pltpu.VMEM((tm, tc), F32)]
    if sample:
        in_specs.append(pl.BlockSpec((hist_rows, tc), lambda i, c: (0, _prev(c))))
        args.append(hist)
        tail_shape = jax.ShapeDtypeStruct((hist_rows, d_a), F32)
        tail_spec = pl.BlockSpec((hist_rows, tc), lambda i, c: (0, _prev(c)))
    else:
        scratch.append(pltpu.VMEM((n_c, hist_rows, tc), F32))
        tail_shape = jax.ShapeDtypeStruct((n_i, hist_rows, d_a), F32)
        tail_spec = pl.BlockSpec((None, hist_rows, tc), lambda i, c: (i, 0, _prev(c)))
    return pl.pallas_call(
        functools.partial(_sconv_body, tm=tm, n_c=n_c, hist_rows=hist_rows, shift=shift,
                          tiles_per_seq=tiles_per_seq, sample=sample),
        grid=(n_i, n_c + 1),
        in_specs=in_specs,
        out_specs=[pl.BlockSpec((tm, tc), lambda i, c: (i, _prev(c))), tail_spec],
        out_shape=[jax.ShapeDtypeStruct((n, d_a), BF16), tail_shape],
        scratch_shapes=scratch,
        compiler_params=_cparams(("arbitrary", "arbitrary")),
        name="short_conv",
    )(*args)


def _glu_conv_body(*refs, tm, n_c, hist_rows, shift, tiles_per_seq, sample):
    if sample:
        (h_ref, wlo_ref, whi_ref, blo_ref, bhi_ref, cw_ref, cb_ref, hist_ref,
         c_ref, tail_ref, f0, f1) = refs
        carry_ref = shifted_ref = None
    else:
        (h_ref, wlo_ref, whi_ref, blo_ref, bhi_ref, cw_ref, cb_ref,
         c_ref, tail_ref, f0, f1, carry_ref, shifted_ref) = refs
        hist_ref = None
    full = (f0, f1)
    i = pl.program_id(0)
    c = pl.program_id(1)
    first = lax.rem(i, tiles_per_seq) == 0

    def prologue():
        if not sample:
            _zero_carries_once(i, (carry_ref,))

    def produce(b):
        h = h_ref[...]
        a_lo = _dot(h, wlo_ref[...]) + blo_ref[...]
        a_hi = _dot(h, whi_ref[...]) + bhi_ref[...]
        full[b][pl.ds(hist_rows, tm), :] = a_lo * _sigmoid(a_hi)

    def stage(b):
        _stage_history(full[b], hist_ref, carry_ref, tail_ref, c - 1, first, hist_rows, tm)

    def finish(b):
        def store(r0, rows, acc):
            c_ref[pl.ds(r0, rows), :] = acc + cb_ref[...]

        if sample:
            _causal_taps_banded(full[b], cw_ref, hist_rows, tm, shift, store)
        else:
            _causal_taps_rowshift(full[b], cw_ref, shifted_ref, hist_rows, tm, store)

    _chunk_pipeline(c, n_c, produce, stage, finish, prologue=prologue)


def _glu_conv(h, w1, b1, conv_w, conv_b, layer, hist, *, tm, tc, shift, tiles_per_seq):
    n, d = h.shape
    _, kw, d_c = conv_w.shape
    n_c = d_c // tc
    n_i = n // tm
    sample = hist is not None
    hist_rows = (kw - 1) * shift if sample else -(-(kw - 1) // V7X_SUBLANES) * V7X_SUBLANES
    cur = lambda c: jnp.minimum(c, n_c - 1)
    in_specs = [
        pl.BlockSpec((tm, d), lambda i, c: (i, 0)),
        pl.BlockSpec((None, d, tc), lambda i, c: (layer, 0, cur(c))),
        pl.BlockSpec((None, d, tc), lambda i, c: (layer, 0, n_c + cur(c))),
        pl.BlockSpec((None, 1, tc), lambda i, c: (layer, 0, cur(c))),
        pl.BlockSpec((None, 1, tc), lambda i, c: (layer, 0, n_c + cur(c))),
        pl.BlockSpec((None, kw, tc), lambda i, c: (layer, 0, _prev(c))),
        pl.BlockSpec((None, 1, tc), lambda i, c: (layer, 0, _prev(c))),
    ]
    b1r = b1.reshape(b1.shape[0], 1, 2 * d_c)
    args = [h, w1, w1, b1r, b1r, conv_w, conv_b.reshape(conv_b.shape[0], 1, d_c)]
    scratch = [pltpu.VMEM((hist_rows + tm, tc), F32), pltpu.VMEM((hist_rows + tm, tc), F32)]
    if sample:
        in_specs.append(pl.BlockSpec((hist_rows, tc), lambda i, c: (0, _prev(c))))
        args.append(hist)
        tail_shape = jax.ShapeDtypeStruct((hist_rows, d_c), F32)
        tail_spec = pl.BlockSpec((hist_rows, tc), lambda i, c: (0, _prev(c)))
    else:
        scratch.append(pltpu.VMEM((n_c, hist_rows, tc), F32))
        scratch.append(pltpu.VMEM((V7X_SUBLANES - 1, tm + hist_rows - V7X_SUBLANES, tc), F32))
        tail_shape = jax.ShapeDtypeStruct((n_i, hist_rows, d_c), F32)
        tail_spec = pl.BlockSpec((None, hist_rows, tc), lambda i, c: (i, 0, _prev(c)))
    return pl.pallas_call(
        functools.partial(_glu_conv_body, tm=tm, n_c=n_c, hist_rows=hist_rows, shift=shift,
                          tiles_per_seq=tiles_per_seq, sample=sample),
        grid=(n_i, n_c + 1),
        in_specs=in_specs,
        out_specs=[pl.BlockSpec((tm, tc), lambda i, c: (i, _prev(c))), tail_spec],
        out_shape=[jax.ShapeDtypeStruct((n, d_c), F32), tail_shape],
        scratch_shapes=scratch,
        compiler_params=_cparams(("arbitrary", "arbitrary")),
        name="glu_conv",
    )(*args)


def _qkv_body(h_ref, w_ref, qg_ref, kg_ref, cos_ref, sin_ref,
              qn_ref, ckv_ref, kpe_ref, kf_ref, *, q_lora, kv_lora):
    z = _dot(h_ref[...], w_ref[...])
    qn_ref[...] = _rms_rows(z[:, :q_lora], qg_ref[...]).astype(qn_ref.dtype)
    ckv = _rms_rows(z[:, q_lora:q_lora + kv_lora], kg_ref[...])
    o = q_lora + kv_lora
    kpe = (z[:, o:o + V7X_LANES] * cos_ref[...]
           + z[:, o + V7X_LANES:o + 2 * V7X_LANES] * sin_ref[...])
    ckv_ref[...] = ckv
    kpe_ref[...] = kpe
    kf_ref[:, :kv_lora] = ckv.astype(kf_ref.dtype)
    kf_ref[:, kv_lora:] = kpe.astype(kf_ref.dtype)


def _qkv_proj(h, w_qkv, q_gain, kv_gain, cos_t, sin_t, *, tm, rope_tiles):
    n, d = h.shape
    q_lora = q_gain.shape[0]
    kv_lora = kv_gain.shape[0]
    wn = w_qkv.shape[1]
    kfw = kv_lora + V7X_LANES
    return pl.pallas_call(
        functools.partial(_qkv_body, q_lora=q_lora, kv_lora=kv_lora),
        grid=(n // tm,),
        in_specs=[
            pl.BlockSpec((tm, d), lambda i: (i, 0)),
            pl.BlockSpec((d, wn), lambda i: (0, 0)),
            pl.BlockSpec((1, q_lora), lambda i: (0, 0)),
            pl.BlockSpec((1, kv_lora), lambda i: (0, 0)),
            pl.BlockSpec((tm, V7X_LANES), lambda i: (i % rope_tiles, 0)),
            pl.BlockSpec((tm, V7X_LANES), lambda i: (i % rope_tiles, 0)),
        ],
        out_specs=[
            pl.BlockSpec((tm, q_lora), lambda i: (i, 0)),
            pl.BlockSpec((tm, kv_lora), lambda i: (i, 0)),
            pl.BlockSpec((tm, V7X_LANES), lambda i: (i, 0)),
            pl.BlockSpec((tm, kfw), lambda i: (i, 0)),
        ],
        out_shape=[
            jax.ShapeDtypeStruct((n, q_lora), BF16),
            jax.ShapeDtypeStruct((n, kv_lora), F32),
            jax.ShapeDtypeStruct((n, V7X_LANES), F32),
            jax.ShapeDtypeStruct((n, kfw), BF16),
        ],
        compiler_params=_cparams(("arbitrary",)),
        name="qkv_proj",
    )(h, w_qkv, q_gain.reshape(1, q_lora), kv_gain.reshape(1, kv_lora), cos_t, sin_t)


def _q_up_body(qn_ref, wn_ref, wp_ref, wr_ref, uk_ref, cos_ref, sin_ref, o_ref, *, hp, nope, kv_lora):
    qn = qn_ref[...]
    q_nope = _dot(qn, wn_ref[...]).astype(BF16)
    pe = _dot(qn, wp_ref[...])
    rot = _dot(qn, wr_ref[...])
    cos = cos_ref[...]
    sin = sin_ref[...]
    for hh in range(hp):
        lat = _dot(q_nope[:, hh * nope:(hh + 1) * nope], uk_ref[hh])
        sl = slice(hh * V7X_LANES, (hh + 1) * V7X_LANES)
        o_ref[hh, :, :kv_lora] = lat.astype(o_ref.dtype)
        o_ref[hh, :, kv_lora:] = (pe[:, sl] * cos + rot[:, sl] * sin).astype(o_ref.dtype)


def _q_up(qn, w_nope, w_pe, w_rot, w_ukt, cos_t, sin_t, *, tm, rope_tiles, hp):
    n, q_lora = qn.shape
    n_heads, nope, kv_lora = w_ukt.shape
    qw = kv_lora + V7X_LANES
    return pl.pallas_call(
        functools.partial(_q_up_body, hp=hp, nope=nope, kv_lora=kv_lora),
        grid=(n // tm, n_heads // hp),
        in_specs=[
            pl.BlockSpec((tm, q_lora), lambda i, g: (i, 0)),
            pl.BlockSpec((q_lora, hp * nope), lambda i, g: (0, g)),
            pl.BlockSpec((q_lora, hp * V7X_LANES), lambda i, g: (0, g)),
            pl.BlockSpec((q_lora, hp * V7X_LANES), lambda i, g: (0, g)),
            pl.BlockSpec((hp, nope, kv_lora), lambda i, g: (g, 0, 0)),
            pl.BlockSpec((tm, V7X_LANES), lambda i, g: (i % rope_tiles, 0)),
            pl.BlockSpec((tm, V7X_LANES), lambda i, g: (i % rope_tiles, 0)),
        ],
        out_specs=pl.BlockSpec((hp, tm, qw), lambda i, g: (g, i, 0)),
        out_shape=jax.ShapeDtypeStruct((n_heads, n, qw), BF16),
        compiler_params=_cparams(("arbitrary", "arbitrary")),
        name="q_up",
    )(qn, w_nope, w_pe, w_rot, w_ukt, cos_t, sin_t)


def _lane_tiled(x, width):
    reps = width // V7X_LANES
    return x if reps == 1 else jnp.concatenate([x] * reps, axis=-1)


def _softmax_update(s, v, m_ref, l_ref, acc_ref):
    tk = s.shape[1]
    m_prev = m_ref[...]
    m_new = jnp.maximum(m_prev, jnp.max(s, axis=-1, keepdims=True))
    alpha = jnp.exp(m_prev - m_new)
    l_part = alpha * l_ref[...]
    if tk % V7X_LANES == 0:
        p = jnp.exp(s - _lane_tiled(m_new, tk))
        for t in range(tk // V7X_LANES):
            l_part = l_part + p[:, t * V7X_LANES:(t + 1) * V7X_LANES]
    else:
        p = jnp.exp(s - m_new[:, :1])
        lane = lax.broadcasted_iota(jnp.int32, l_part.shape, 1)
        l_part = l_part + jnp.where(lane == 0, jnp.sum(p, axis=-1, keepdims=True), 0.0)
    l_ref[...] = l_part
    acc_ref[...] = _lane_tiled(alpha, acc_ref.shape[1]) * acc_ref[...] + _dot(p.astype(BF16), v)
    m_ref[...] = m_new


def _softmax_denominator(l_ref):
    return jnp.sum(l_ref[...], axis=-1, keepdims=True)


def _attn_prompt_body(q_ref, k_ref, wv_ref, o_ref, m_ref, l_ref, acc_ref,
                      *, tq, tk, n_k, n_heads, kv_lora, v_dim, scale):
    qi = pl.program_id(1)
    ki = pl.program_id(2)
    first_masked = (qi * tq) // tk
    last = ((qi + 1) * tq - 1) // tk
    rows = n_heads * tq

    @pl.when(ki == 0)
    def _():
        m_ref[...] = jnp.full(m_ref.shape, -jnp.inf, F32)
        l_ref[...] = jnp.zeros(l_ref.shape, F32)
        acc_ref[...] = jnp.zeros(acc_ref.shape, F32)

    def scores():
        q = q_ref[...].reshape(rows, q_ref.shape[-1])
        return _dot_nt(q, k_ref[...]) * scale

    @pl.when(ki < first_masked)
    def _():
        _softmax_update(scores(), k_ref[:, :kv_lora], m_ref, l_ref, acc_ref)

    @pl.when(jnp.logical_and(ki >= first_masked, ki <= last))
    def _():
        s = scores()
        q_pos = qi * tq + lax.rem(lax.broadcasted_iota(jnp.int32, (rows, tk), 0), tq)
        k_pos = ki * tk + lax.broadcasted_iota(jnp.int32, (rows, tk), 1)
        s = jnp.where(k_pos <= q_pos, s, -jnp.inf)
        _softmax_update(s, k_ref[:, :kv_lora], m_ref, l_ref, acc_ref)

    @pl.when(ki == n_k - 1)
    def _():
        o_lat = (acc_ref[...] / _softmax_denominator(l_ref)).astype(BF16)
        for hh in range(n_heads):
            o_ref[:, hh * v_dim:(hh + 1) * v_dim] = _dot(
                o_lat[hh * tq:(hh + 1) * tq], wv_ref[hh]).astype(o_ref.dtype)


def _attn_prompt(q, kf, w_uvh, *, batch, seq, tq, tk, scale):
    n_heads, n, qw = q.shape
    _, kv_lora, v_dim = w_uvh.shape
    n_q = seq // tq
    n_k = seq // tk
    rows = n_heads * tq

    def k_map(b, qi, ki):
        last = ((qi + 1) * tq - 1) // tk
        return (b * n_k + jnp.minimum(ki, last), 0)

    return pl.pallas_call(
        functools.partial(_attn_prompt_body, tq=tq, tk=tk, n_k=n_k, n_heads=n_heads,
                          kv_lora=kv_lora, v_dim=v_dim, scale=scale),
        grid=(batch, n_q, n_k),
        in_specs=[
            pl.BlockSpec((n_heads, tq, qw), lambda b, qi, ki: (0, b * n_q + qi, 0)),
            pl.BlockSpec((tk, qw), k_map),
            pl.BlockSpec((n_heads, kv_lora, v_dim), lambda b, qi, ki: (0, 0, 0)),
        ],
        out_specs=pl.BlockSpec((tq, n_heads * v_dim), lambda b, qi, ki: (b * n_q + qi, 0)),
        out_shape=jax.ShapeDtypeStruct((n, n_heads * v_dim), BF16),
        scratch_shapes=[pltpu.VMEM((rows, V7X_LANES), F32), pltpu.VMEM((rows, V7X_LANES), F32),
                        pltpu.VMEM((rows, kv_lora), F32)],
        compiler_params=_cparams(("arbitrary", "arbitrary", "arbitrary")),
        name="attn_prompt",
    )(q, kf, w_uvh)


def _attn_sample_body(pt_ref, q_ref, kn_ref, ckv_hbm, kpet_hbm, o_ref,
                      cbuf, pbuf, sem, m_ref, l_ref, acc_ref,
                      *, layer, pages, page, n_half, n_heads, n_new, kv_lora, rope, scale):
    b = pl.program_id(0)
    hf = pl.program_id(1)
    step = b * n_half + hf
    n_steps = pl.num_programs(0) * n_half
    slot = lax.rem(step, 2)

    def page_copies(bb, hh, sl):
        cps = []
        for p in range(pages):
            pg = pt_ref[bb, hh * pages + p]
            keys = pl.ds(p * page, page)
            cps.append(pltpu.make_async_copy(ckv_hbm.at[layer, pg], cbuf.at[sl, keys, :], sem.at[0, sl]))
            cps.append(pltpu.make_async_copy(kpet_hbm.at[layer, pg], pbuf.at[sl, :, keys], sem.at[1, sl]))
        return cps

    @pl.when(step == 0)
    def _():
        for cp in page_copies(b, hf, slot):
            cp.start()

    @pl.when(step + 1 < n_steps)
    def _():
        wrap = hf == n_half - 1
        nb = jnp.where(wrap, b + 1, b)
        nh = jnp.where(wrap, 0, hf + 1)
        for cp in page_copies(nb, nh, 1 - slot):
            cp.start()

    for cp in page_copies(b, hf, slot):
        cp.wait()

    @pl.when(hf == 0)
    def _():
        m_ref[...] = jnp.full(m_ref.shape, -jnp.inf, F32)
        l_ref[...] = jnp.zeros(l_ref.shape, F32)
        acc_ref[...] = jnp.zeros(acc_ref.shape, F32)

    q = q_ref[0]
    kc = cbuf[slot].astype(BF16)
    kpt = pbuf[slot].astype(BF16)
    s = (_dot_nt(q[:, :kv_lora], kc) + _dot(q[:, kv_lora:kv_lora + rope], kpt)) * scale
    _softmax_update(s, kc, m_ref, l_ref, acc_ref)

    @pl.when(hf == n_half - 1)
    def _():
        kn = kn_ref[0]
        sn = _dot_nt(q, kn) * scale
        t_row = lax.broadcasted_iota(jnp.int32, sn.shape, 0) // n_heads
        u_col = lax.broadcasted_iota(jnp.int32, sn.shape, 1)
        sn = jnp.where(jnp.logical_and(u_col <= t_row, u_col < n_new), sn, -jnp.inf)
        _softmax_update(sn, kn[:, :kv_lora], m_ref, l_ref, acc_ref)
        o_ref[0] = (acc_ref[...] / _softmax_denominator(l_ref)).astype(o_ref.dtype)


def _attn_sample(page_table, q, k_new, cache_ckv, cache_kpe_t, *, layer, n_new, n_heads, scale):
    bd, rows, qw = q.shape
    _, _, page, kv_lora = cache_ckv.shape
    rope = cache_kpe_t.shape[2]
    n_pages = page_table.shape[1]
    past = n_pages * page
    keys = min(SAMPLE_KEYS_PER_STEP, past)
    n_half = past // keys
    pages = keys // page
    grid_spec = pltpu.PrefetchScalarGridSpec(
        num_scalar_prefetch=1,
        grid=(bd, n_half),
        in_specs=[
            pl.BlockSpec((1, rows, qw), lambda b, hf, pt: (b, 0, 0)),
            pl.BlockSpec((1, k_new.shape[1], qw), lambda b, hf, pt: (b, 0, 0)),
            pl.BlockSpec(memory_space=pl.ANY),
            pl.BlockSpec(memory_space=pl.ANY),
        ],
        out_specs=pl.BlockSpec((1, rows, kv_lora), lambda b, hf, pt: (b, 0, 0)),
        scratch_shapes=[
            pltpu.VMEM((2, keys, kv_lora), F32),
            pltpu.VMEM((2, rope, keys), F32),
            pltpu.SemaphoreType.DMA((2, 2)),
            pltpu.VMEM((rows, V7X_LANES), F32),
            pltpu.VMEM((rows, V7X_LANES), F32),
            pltpu.VMEM((rows, kv_lora), F32),
        ],
    )
    return pl.pallas_call(
        functools.partial(_attn_sample_body, layer=layer, pages=pages, page=page, n_half=n_half,
                          n_heads=n_heads, n_new=n_new, kv_lora=kv_lora, rope=rope, scale=scale),
        grid_spec=grid_spec,
        out_shape=jax.ShapeDtypeStruct((bd, rows, kv_lora), BF16),
        compiler_params=_cparams(("arbitrary", "arbitrary")),
        name="attn_sample",
    )(page_table, q, k_new, cache_ckv, cache_kpe_t)


def _head_proj_body(x_ref, w_ref, o_ref):
    o_ref[...] = _dot(x_ref[0], w_ref[0]).astype(o_ref.dtype)


def _head_proj(o_lat, w_uvh):
    n_heads, n, kv_lora = o_lat.shape
    v_dim = w_uvh.shape[-1]
    return pl.pallas_call(
        _head_proj_body,
        grid=(n_heads,),
        in_specs=[pl.BlockSpec((1, n, kv_lora), lambda h: (h, 0, 0)),
                  pl.BlockSpec((1, kv_lora, v_dim), lambda h: (h, 0, 0))],
        out_specs=pl.BlockSpec((n, v_dim), lambda h: (0, h)),
        out_shape=jax.ShapeDtypeStruct((n, n_heads * v_dim), BF16),
        compiler_params=_cparams(("arbitrary",)),
        name="head_proj",
    )(o_lat, w_uvh)


def _rope_tables(pos, rope):
    half = rope // 2
    inv = 1.0 / (ROPE_THETA ** (jnp.arange(half, dtype=F32) * (2.0 / rope)))
    ang = pos.astype(F32)[:, None] * inv[None, :]
    cos = jnp.cos(ang)
    sin = jnp.sin(ang)
    pad = jnp.zeros((pos.shape[0], V7X_LANES - rope), F32)
    return (jnp.concatenate([cos, cos, pad], axis=-1), jnp.concatenate([-sin, sin, pad], axis=-1))


def _swap_halves(w):
    half = w.shape[-1] // 2
    return jnp.concatenate([w[..., half:], w[..., :half]], axis=-1)


def _pad_lanes(w):
    pad = [(0, 0)] * (w.ndim - 1) + [(0, V7X_LANES - w.shape[-1])]
    return jnp.pad(w, pad)


def _time_major(a):
    b, t, c = a.shape
    return jnp.swapaxes(a, 0, 1).reshape(t * b, c)


def _batch_major(a, b):
    tb, c = a.shape
    return jnp.swapaxes(a.reshape(tb // b, b, c), 0, 1)


def kernel(x_prompt, x_sample, cache_ckv, cache_kpe, page_table, state_sconv, state_conf, state_ffn,
           norm_mix, norm_ffn, norm_final, w_in_e, conv_a, q_norm, w_uq, kv_norm, w_ukv, w_out_e,
           w_pw1, b_pw1, conv_c, conv_c_b, ln_c_g, ln_c_b, w_pw2, b_pw2, w_up, conv_f, w_down):
    bp, seq, d = x_prompt.shape
    bd, t_len, _ = x_sample.shape
    depth = norm_mix.shape[0]
    d_a = conv_a.shape[-1]
    q_lora = q_norm.shape[-1]
    kv_lora = kv_norm.shape[-1]
    rope = cache_kpe.shape[-1]
    n_heads = w_ukv.shape[2]
    nope = w_uq.shape[-1] // n_heads - rope
    v_dim = w_ukv.shape[-1] - nope
    page = cache_ckv.shape[2]
    past_len = page_table.shape[1] * page
    scale = float(nope + rope) ** -0.5
    n_p = bp * seq
    n_s = bd * t_len

    tm_p = min(TOKEN_TILE, seq)
    tps = seq // tm_p
    tq = min(ATTN_Q_TILE, seq)
    tk = min(ATTN_K_TILE, seq)
    tf = min(FF_TILE, w_down.shape[1])
    tc_a = min(CH_TILE, d_a)
    tc_c = min(CH_TILE, d)
    tn = min(OUT_TILE, d)

    cos_p, sin_p = _rope_tables(jnp.arange(seq), rope)
    cos_s, sin_s = _rope_tables(past_len + jnp.repeat(jnp.arange(t_len), bd), rope)

    tm_q = Q_UP_TILE if seq % Q_UP_TILE == 0 else tm_p
    groups = (
        dict(n=n_p, tm=tm_p, shift=1, tps=tps, cos=cos_p, sin=sin_p, rope_tiles=tps,
             tm_q=tm_q, rope_tiles_q=seq // tm_q),
        dict(n=n_s, tm=n_s, shift=bd, tps=1, cos=cos_s, sin=sin_s, rope_tiles=1,
             tm_q=n_s, rope_tiles_q=1),
    )
    xs = [x_prompt.reshape(n_p, d), _time_major(x_sample)]
    hs = [_rmsnorm(x, norm_mix[0], g["tm"]) for x, g in zip(xs, groups)]

    w_in_b = w_in_e.astype(BF16)
    w_out_b = w_out_e.astype(BF16)
    w_pw1_b = w_pw1.astype(BF16)
    w_pw2_b = w_pw2.astype(BF16)
    w_up_b = w_up.astype(BF16)
    w_down_b = w_down.astype(BF16)
    cache_kpe_t = jnp.swapaxes(cache_kpe, 2, 3)
    zero_bias = jnp.zeros((d,), F32)

    sc_new, cf_new, ff_new, ckvs, kpes = [[], []], [[], []], [[], []], [[], []], [[], []]

    def prompt_tail(tail, rows):
        return tail[tps - 1::tps, tail.shape[1] - rows:, :]

    for l in range(depth):
        nxt_gain = norm_ffn[l]
        if l % 2 == 0:
            e = l // 2
            w_in = w_in_e[e]
            o = 3 * d_a
            w_kr = w_in[:, o + q_lora + kv_lora:]
            w_qkv = jnp.concatenate(
                [w_in[:, o:o + q_lora + kv_lora], _pad_lanes(w_kr), _pad_lanes(_swap_halves(w_kr))],
                axis=-1).astype(BF16)
            wq = w_uq[e].reshape(q_lora, n_heads, nope + rope)
            w_nope = wq[..., :nope].reshape(q_lora, n_heads * nope).astype(BF16)
            w_pe = _pad_lanes(wq[..., nope:]).reshape(q_lora, n_heads * V7X_LANES).astype(BF16)
            w_rot = _pad_lanes(_swap_halves(wq[..., nope:])).reshape(q_lora, n_heads * V7X_LANES).astype(BF16)
            w_ukt = jnp.transpose(w_ukv[e][..., :nope], (1, 2, 0)).astype(BF16)
            w_uvh = jnp.transpose(w_ukv[e][..., nope:], (1, 0, 2)).astype(BF16)
            for gi, g in enumerate(groups):
                hist = None if gi == 0 else _time_major(state_sconv[e])
                ya, tail = _short_conv(hs[gi], w_in_b, conv_a, e, hist, d_a=d_a, tm=g["tm"], tc=tc_a,
                                       shift=g["shift"], tiles_per_seq=g["tps"])
                qn, ckv, kpe, kf = _qkv_proj(hs[gi], w_qkv, q_norm[e], kv_norm[e], g["cos"], g["sin"],
                                             tm=g["tm"], rope_tiles=g["rope_tiles"])
                qf = _q_up(qn, w_nope, w_pe, w_rot, w_ukt, g["cos"], g["sin"],
                           tm=g["tm_q"], rope_tiles=g["rope_tiles_q"], hp=2)
                if gi == 0:
                    o_mix = _attn_prompt(qf, kf, w_uvh, batch=bp, seq=seq, tq=tq, tk=tk, scale=scale)
                    sc_new[gi].append(prompt_tail(tail, 2))
                    ckvs[gi].append(ckv.reshape(bp, seq, kv_lora))
                    kpes[gi].append(kpe[:, :rope].reshape(bp, seq, rope))
                else:
                    qw = qf.shape[-1]
                    q_seq = jnp.transpose(qf.reshape(n_heads, t_len, bd, qw), (2, 1, 0, 3))
                    q_seq = q_seq.reshape(bd, t_len * n_heads, qw)
                    k_new = jnp.pad(_batch_major(kf, bd), ((0, 0), (0, 2 * V7X_SUBLANES - t_len), (0, 0)))
                    o_lat = _attn_sample(page_table, q_seq, k_new, cache_ckv, cache_kpe_t, layer=e,
                                         n_new=t_len, n_heads=n_heads, scale=scale)
                    o_lat = jnp.transpose(o_lat.reshape(bd, t_len, n_heads, kv_lora), (2, 1, 0, 3))
                    o_mix = _head_proj(o_lat.reshape(n_heads, n_s, kv_lora), w_uvh)
                    sc_new[gi].append(_batch_major(tail, bd))
                    ckvs[gi].append(_batch_major(ckv, bd))
                    kpes[gi].append(_batch_major(kpe[:, :rope], bd))
                xs[gi], hs[gi] = _linear_residual_norm(
                    [ya, o_mix], w_out_b, e, zero_bias, xs[gi], nxt_gain, tm=g["tm"], tn=tn)
        else:
            o = l // 2
            kw = conv_c.shape[1]
            for gi, g in enumerate(groups):
                hist = None if gi == 0 else _time_major(state_conf[o])
                cpre, tail = _glu_conv(hs[gi], w_pw1_b, b_pw1, conv_c, conv_c_b, o, hist,
                                       tm=g["tm"], tc=tc_c, shift=g["shift"], tiles_per_seq=g["tps"])
                act = _ln_silu(cpre, ln_c_g[o], ln_c_b[o], g["tm"])
                cf_new[gi].append(prompt_tail(tail, kw - 1) if gi == 0 else _batch_major(tail, bd))
                xs[gi], hs[gi] = _linear_residual_norm(
                    [act], w_pw2_b, o, b_pw2[o], xs[gi], nxt_gain, tm=g["tm"], tn=tn)
        final = l == depth - 1
        gain = norm_final if final else norm_mix[l + 1]
        for gi, g in enumerate(groups):
            hist = None if gi == 0 else _time_major(state_ffn[l])
            xs[gi], hs[gi], tail_g, tail_u = _conv_ffn(
                hs[gi], xs[gi], w_up_b, conv_f, w_down_b, l, gain, hist, tm=g["tm"], tf=tf,
                shift=g["shift"], tiles_per_seq=g["tps"], out_dtype=F32 if final else BF16)
            if gi == 0:
                ff_new[gi].append(jnp.concatenate([prompt_tail(tail_g, 2), prompt_tail(tail_u, 2)], axis=-1))
            else:
                ff_new[gi].append(_batch_major(jnp.concatenate([tail_g, tail_u], axis=-1), bd))

    y_prompt = hs[0].reshape(bp, seq, d)
    y_sample = _batch_major(hs[1], bd)
    st = lambda parts: jnp.stack(parts)
    return (y_prompt, y_sample, st(ckvs[0]), st(kpes[0]), st(ckvs[1]), st(kpes[1]),
            st(sc_new[0]), st(sc_new[1]), st(cf_new[0]), st(cf_new[1]), st(ff_new[0]), st(ff_new[1]))
```
